```python
import jax, jax.numpy as jnp
from jax import lax
import numpy as np

D_MODEL = 1024
BATCH = 16
SEQ = 4096
DEPTH = 1

D_MIX = D_MODEL
SGU_WIDTH = D_MIX // 2
SGU_GROUPS = 4
SGU_GROUP_DIM = SGU_WIDTH // SGU_GROUPS
SGU_CHUNK = 128
DN_WIDTH = D_MIX - SGU_WIDTH
DN_HEADS = 4
DN_HEAD_DIM = DN_WIDTH // DN_HEADS
DN_CHUNK = 64
CONV_K = 4
PLE_DIM = 256
EPS = 1e-6

IN_SIZES = (SGU_WIDTH, SGU_WIDTH, SGU_WIDTH,
            DN_WIDTH, DN_WIDTH, DN_WIDTH, DN_WIDTH,
            DN_HEADS, DN_HEADS)
IN_COLS = sum(IN_SIZES)
IN_SPLITS = tuple(int(s) for s in np.cumsum(IN_SIZES)[:-1])

kernel_name = "hybrid_sgu_gated_deltanet_ple"


def rms_norm(x, g):
    xf = x.astype(jnp.float32)
    y = xf * lax.rsqrt(jnp.mean(xf * xf, axis=-1, keepdims=True) + EPS)
    return (y * g.astype(jnp.float32)).astype(x.dtype)


def layer_norm(x, g, b):
    xf = x.astype(jnp.float32)
    mu = jnp.mean(xf, axis=-1, keepdims=True)
    xc = xf - mu
    y = xc * lax.rsqrt(jnp.mean(xc * xc, axis=-1, keepdims=True) + EPS)
    return (y * g.astype(jnp.float32) + b.astype(jnp.float32)).astype(x.dtype)


def l2_norm(x):
    return x * lax.rsqrt(jnp.sum(x * x, axis=-1, keepdims=True) + EPS)


def chunked_sgu(u, v, ln_g, ln_b, w_s, b_s):
    B, S, _ = u.shape
    nc = S // SGU_CHUNK
    u = jax.nn.gelu(u, approximate=False)
    v = jax.nn.gelu(v, approximate=False)
    v = v.reshape(B, nc, SGU_CHUNK, SGU_GROUPS, SGU_GROUP_DIM)
    v = layer_norm(v, ln_g.reshape(SGU_GROUPS, SGU_GROUP_DIM), ln_b.reshape(SGU_GROUPS, SGU_GROUP_DIM))
    causal = jnp.tril(jnp.ones((SGU_CHUNK, SGU_CHUNK), dtype=bool))
    w = jnp.where(causal, w_s, jnp.zeros_like(w_s)).astype(v.dtype)
    s = jnp.einsum('gts,bnsgd->bntgd', w, v) + b_s.T.astype(v.dtype)[:, :, None]
    return u * s.reshape(B, S, SGU_WIDTH)


def causal_depthwise_conv(x, w):
    ch = x.shape[-1]
    return lax.conv_general_dilated(
        x, w[:, None, :].astype(x.dtype), window_strides=(1,), padding=[(CONV_K - 1, 0)],
        dimension_numbers=('NWC', 'WIO', 'NWC'), feature_group_count=ch)


def gated_delta_rule(q, k, v, beta, g):
    B, S, H, dk = q.shape
    dv = v.shape[-1]
    C = DN_CHUNK
    N = S // C
    q = q * (dk ** -0.5)

    def to_chunks(t):
        t = t.reshape((B, N, C, H) + t.shape[3:])
        return jnp.moveaxis(t, 3, 1)

    q, k, v, beta, g = (to_chunks(t) for t in (q, k, v, beta, g))
    g = jnp.cumsum(g, axis=-1)
    tril = jnp.tril(jnp.ones((C, C), dtype=bool))
    strict = jnp.tril(jnp.ones((C, C), dtype=bool), -1)
    diff = g[..., :, None] - g[..., None, :]
    decay = jnp.exp(jnp.where(tril, diff, -jnp.inf))

    k_beta = k * beta[..., None]
    kk = jnp.einsum('bhnid,bhnjd->bhnij', k_beta, k) * decay
    a_mat = jnp.eye(C, dtype=q.dtype) + jnp.where(strict, kk, jnp.zeros_like(kk))
    rhs = jnp.concatenate([v * beta[..., None], k_beta * jnp.exp(g)[..., None]], axis=-1)
    sol = lax.linalg.triangular_solve(a_mat, rhs, left_side=True, lower=True, unit_diagonal=True)
    u_val, w_dec = sol[..., :dv], sol[..., dv:]

    qk = jnp.einsum('bhnid,bhnjd->bhnij', q, k) * decay
    q_dec = q * jnp.exp(g)[..., None]
    g_last = g[..., -1]
    k_dec = k * jnp.exp(g_last[..., None] - g)[..., None]

    xs = tuple(jnp.moveaxis(t, 2, 0) for t in (q_dec, k_dec, u_val, w_dec, qk, g_last))

    def step(state, inp):
        qd, kd, uv, wd, a, gl = inp
        v_new = uv - jnp.einsum('bhcd,bhde->bhce', wd, state)
        out = jnp.einsum('bhcd,bhde->bhce', qd, state) + jnp.einsum('bhij,bhje->bhie', a, v_new)
        state = state * jnp.exp(gl)[..., None, None] + jnp.einsum('bhcd,bhce->bhde', kd, v_new)
        return state, out

    state0 = jnp.zeros((B, H, dk, dv), dtype=q.dtype)
    _, out = lax.scan(step, state0, xs)
    out = jnp.moveaxis(out, 0, 2)
    return jnp.moveaxis(out, 1, 3).reshape(B, S, H, dv)


def gated_deltanet(q, k, v, z, b_logit, a_logit, conv_w, a_log, dt_bias, o_norm_g):
    B, S, _ = q.shape
    dt = q.dtype
    qkv = jax.nn.silu(causal_depthwise_conv(jnp.concatenate([q, k, v], axis=-1), conv_w))
    q, k, v = jnp.split(qkv, 3, axis=-1)
    heads = lambda t: t.reshape(B, S, DN_HEADS, DN_HEAD_DIM).astype(jnp.float32)
    q, k, v = l2_norm(heads(q)), l2_norm(heads(k)), heads(v)
    beta = jax.nn.sigmoid(b_logit.astype(jnp.float32))
    g = -jnp.exp(a_log.astype(jnp.float32)) * jax.nn.softplus(a_logit.astype(jnp.float32) + dt_bias.astype(jnp.float32))
    o = gated_delta_rule(q, k, v, beta, g)
    o = rms_norm(o, o_norm_g).reshape(B, S, DN_WIDTH)
    return (o * jax.nn.silu(z.astype(jnp.float32))).astype(dt)


def setup_inputs(seed: int = 0) -> dict:
    key = jax.random.key(seed)
    ks = jax.random.split(key, 20)
    nrm = lambda k, shape, scale: jax.random.normal(k, shape, jnp.float32) * scale
    x = jax.random.normal(ks[0], (BATCH, SEQ, D_MODEL), jnp.float32)
    p = jax.random.normal(ks[1], (DEPTH, BATCH, SEQ, PLE_DIM), jnp.float32)
    norm_g = 1.0 + nrm(ks[2], (DEPTH, D_MODEL), 0.02)
    w_in = nrm(ks[3], (DEPTH, D_MODEL, IN_COLS), D_MODEL ** -0.5)
    sgu_ln_g = 1.0 + nrm(ks[4], (DEPTH, SGU_WIDTH), 0.02)
    sgu_ln_b = nrm(ks[5], (DEPTH, SGU_WIDTH), 0.02)
    sgu_w_s = nrm(ks[6], (DEPTH, SGU_GROUPS, SGU_CHUNK, SGU_CHUNK), 0.5 * SGU_CHUNK ** -0.5)
    sgu_b_s = 1.0 + nrm(ks[7], (DEPTH, SGU_GROUPS, SGU_CHUNK), 0.01)
    dn_conv_w = nrm(ks[8], (DEPTH, CONV_K, 3 * DN_WIDTH), CONV_K ** -0.5)
    dn_a_log = jnp.log(jax.random.uniform(ks[9], (DEPTH, DN_HEADS), jnp.float32, 1.0, 16.0))
    dt0 = jnp.exp(jax.random.uniform(ks[10], (DEPTH, DN_HEADS), jnp.float32, np.log(1e-3), np.log(1e-1)))
    dn_dt_bias = jnp.log(jnp.expm1(dt0))
    dn_o_norm_g = 1.0 + nrm(ks[11], (DEPTH, DN_HEAD_DIM), 0.02)
    w_out = nrm(ks[12], (DEPTH, D_MIX, D_MODEL), D_MIX ** -0.5)
    ple_norm_g = 1.0 + nrm(ks[13], (DEPTH, D_MODEL), 0.02)
    ple_gate_w = nrm(ks[14], (DEPTH, D_MODEL, D_MODEL), D_MODEL ** -0.5)
    ple_proj_w = nrm(ks[15], (DEPTH, PLE_DIM, D_MODEL), PLE_DIM ** -0.5)
    final_norm_g = 1.0 + nrm(ks[16], (D_MODEL,), 0.02)
    return {"x": x, "p": p, "norm_g": norm_g, "w_in": w_in,
            "sgu_ln_g": sgu_ln_g, "sgu_ln_b": sgu_ln_b, "sgu_w_s": sgu_w_s, "sgu_b_s": sgu_b_s,
            "dn_conv_w": dn_conv_w, "dn_a_log": dn_a_log, "dn_dt_bias": dn_dt_bias,
            "dn_o_norm_g": dn_o_norm_g, "w_out": w_out,
            "ple_norm_g": ple_norm_g, "ple_gate_w": ple_gate_w, "ple_proj_w": ple_proj_w,
            "final_norm_g": final_norm_g}


def reference(x, p, norm_g, w_in, sgu_ln_g, sgu_ln_b, sgu_w_s, sgu_b_s, dn_conv_w, dn_a_log,
              dn_dt_bias, dn_o_norm_g, w_out, ple_norm_g, ple_gate_w, ple_proj_w, final_norm_g):
    h = x
    for i in range(DEPTH):
        xn = rms_norm(h, norm_g[i])
        proj = jnp.einsum('bsd,dc->bsc', xn, w_in[i])
        (a_u, a_v, a_z, b_q, b_k, b_v, b_z, b_beta, b_a) = jnp.split(proj, IN_SPLITS, axis=-1)
        a_out = chunked_sgu(a_u, a_v, sgu_ln_g[i], sgu_ln_b[i], sgu_w_s[i], sgu_b_s[i]) * jax.nn.silu(a_z)
        b_out = gated_deltanet(b_q, b_k, b_v, b_z, b_beta, b_a, dn_conv_w[i], dn_a_log[i],
                               dn_dt_bias[i], dn_o_norm_g[i])
        mixed = jnp.concatenate([a_out, b_out], axis=-1)
        h = h + jnp.einsum('bsc,cd->bsd', mixed, w_out[i])
        gate = jax.nn.sigmoid(jnp.einsum('bsd,de->bse', rms_norm(h, ple_norm_g[i]), ple_gate_w[i]))
        h = h + gate * jnp.einsum('bse,ed->bsd', p[i].astype(h.dtype), ple_proj_w[i])
    return rms_norm(h, final_norm_g)
```

```python
import functools

import jax
import jax.numpy as jnp
from jax import lax
from jax.experimental import pallas as pl
from jax.experimental.pallas import tpu as pltpu

F32 = jnp.float32
BF16 = jnp.bfloat16

EPS = 1e-6
LANES = 128
SGU_GROUPS = 4
SGU_CHUNK = 128
DN_HEADS = 4
DN_HEAD_DIM = 128
CONV_K = 4
DN_BLOCK = 128
TAIL_ROWS = 8
VMEM_LIMIT = 48 * 1024 * 1024

_SQRT_HALF = 0.7071067811865476


def _sigmoid(x):
    return 1.0 / (1.0 + jnp.exp(-x))


def _silu(x):
    return x * _sigmoid(x)


def _gelu(x):
    return 0.5 * x * (1.0 + lax.erf(x * _SQRT_HALF))


def _softplus(x):
    return jnp.maximum(x, 0.0) + jnp.log1p(jnp.exp(-jnp.abs(x)))


def _rms_norm(x, g):
    return x * lax.rsqrt(jnp.mean(x * x, axis=-1, keepdims=True) + EPS) * g


def _dot(a, b):
    return jnp.dot(a.astype(BF16), b.astype(BF16), preferred_element_type=F32)


def _dot_nt(a, b):
    return lax.dot_general(a.astype(BF16), b.astype(BF16), (((1,), (1,)), ((), ())),
                           preferred_element_type=F32)


def _dot_exact_lhs(lhs_bf16, x):
    hi = x.astype(BF16)
    r1 = x - hi.astype(F32)
    mid = r1.astype(BF16)
    lo = (r1 - mid.astype(F32)).astype(BF16)
    d = lambda t: jnp.dot(lhs_bf16, t, preferred_element_type=F32)
    return d(hi) + d(mid) + d(lo)


def _inproj_kernel(x_ref, g_ref, w_ref, wba_ref, au_ref, av_ref, az_ref, qkv_ref, bz_ref, ba_ref):
    xn = _rms_norm(x_ref[...], g_ref[...]).astype(BF16)
    col = 0
    for out_ref in (au_ref, av_ref, az_ref, qkv_ref, bz_ref):
        width = out_ref.shape[-1]
        out_ref[...] = jnp.dot(xn, w_ref[:, col:col + width], preferred_element_type=F32).astype(out_ref.dtype)
        col += width
    ba_ref[...] = jnp.dot(xn, wba_ref[...], preferred_element_type=F32)


def _in_projection(x2, norm_g, w_main, w_ba, *, tile):
    tokens, d_model = x2.shape
    sgu_w, dn_w = 512, 512
    widths = (sgu_w, sgu_w, sgu_w, 3 * dn_w, dn_w)
    row = lambda i: (i, 0)
    const = lambda i: (0, 0)
    out_shapes = [jax.ShapeDtypeStruct((tokens, w), BF16) for w in widths]
    out_shapes.append(jax.ShapeDtypeStruct((tokens, LANES), F32))
    out_specs = [pl.BlockSpec((tile, w), row) for w in widths] + [pl.BlockSpec((tile, LANES), row)]
    return pl.pallas_call(
        _inproj_kernel,
        grid=(tokens // tile,),
        in_specs=[pl.BlockSpec((tile, d_model), row),
                  pl.BlockSpec((1, d_model), const),
                  pl.BlockSpec(w_main.shape, const),
                  pl.BlockSpec(w_ba.shape, const)],
        out_specs=out_specs,
        out_shape=out_shapes,
        compiler_params=pltpu.CompilerParams(dimension_semantics=("arbitrary",),
                                             vmem_limit_bytes=VMEM_LIMIT),
        name="in_projection",
    )(x2, norm_g, w_main, w_ba)


def _sgu_kernel(u_ref, v_ref, z_ref, lng_ref, lnb_ref, ws_ref, bst_ref, o_ref):
    rows = u_ref.shape[0]
    r = lax.broadcasted_iota(jnp.int32, (SGU_CHUNK, SGU_CHUNK), 0)
    c = lax.broadcasted_iota(jnp.int32, (SGU_CHUNK, SGU_CHUNK), 1)
    causal = r >= c
    for g in range(SGU_GROUPS):
        lanes = slice(g * LANES, (g + 1) * LANES)
        w = jnp.where(causal, ws_ref[g], 0.0).astype(BF16)
        bias = bst_ref[:, g:g + 1]
        ln_g = lng_ref[:, lanes]
        ln_b = lnb_ref[:, lanes]
        for ci in range(rows // SGU_CHUNK):
            rs = slice(ci * SGU_CHUNK, (ci + 1) * SGU_CHUNK)
            v = _gelu(v_ref[rs, lanes].astype(F32))
            mu = jnp.mean(v, axis=-1, keepdims=True)
            vc = v - mu
            y = vc * lax.rsqrt(jnp.mean(vc * vc, axis=-1, keepdims=True) + EPS) * ln_g + ln_b
            s = jnp.dot(w, y.astype(BF16), preferred_element_type=F32) + bias
            u = _gelu(u_ref[rs, lanes].astype(F32))
            z = z_ref[rs, lanes].astype(F32)
            o_ref[rs, lanes] = (u * s * _silu(z)).astype(o_ref.dtype)


def _sgu(au, av, az, ln_g, ln_b, w_s, b_s_t, *, tile):
    tokens, width = au.shape
    row = lambda i: (i, 0)
    const2 = lambda i: (0, 0)
    return pl.pallas_call(
        _sgu_kernel,
        grid=(tokens // tile,),
        in_specs=[pl.BlockSpec((tile, width), row)] * 3 + [
            pl.BlockSpec((1, width), const2),
            pl.BlockSpec((1, width), const2),
            pl.BlockSpec(w_s.shape, lambda i: (0, 0, 0)),
            pl.BlockSpec(b_s_t.shape, const2)],
        out_specs=pl.BlockSpec((tile, width), row),
        out_shape=jax.ShapeDtypeStruct((tokens, width), BF16),
        compiler_params=pltpu.CompilerParams(dimension_semantics=("arbitrary",),
                                             vmem_limit_bytes=VMEM_LIMIT),
        name="sgu",
    )(au, av, az, ln_g, ln_b, w_s, b_s_t)


def _unit_lower_inverse(strict_lower):
    n = strict_lower.shape[0]
    r = lax.broadcasted_iota(jnp.int32, (n, n), 0)
    c = lax.broadcasted_iota(jnp.int32, (n, n), 1)
    m = -strict_lower
    inv = jnp.where(r == c, 1.0, 0.0) + m
    power = m
    span = 2
    while span < n:
        power = _dot(power, power)
        inv = inv + _dot(inv, power)
        span *= 2
    return inv


def _deltanet_kernel(qkv_ref, z_ref, ba_ref, convw_ref, alog_ref, dtb_ref, og_ref, o_ref,
                     xbuf_ref, act_ref, state_ref):
    t = pl.program_id(1)
    ts = qkv_ref.shape[1]
    width = DN_HEADS * DN_HEAD_DIM

    @pl.when(t == 0)
    def _():
        xbuf_ref[0:TAIL_ROWS, :] = jnp.zeros((TAIL_ROWS, xbuf_ref.shape[1]), F32)
        state_ref[...] = jnp.zeros(state_ref.shape, F32)

    for j in range(3 * DN_HEADS):
        lanes = slice(j * LANES, (j + 1) * LANES)
        x = qkv_ref[0, :, lanes].astype(F32)
        xbuf_ref[TAIL_ROWS:TAIL_ROWS + ts, lanes] = x
        y = convw_ref[CONV_K - 1:CONV_K, lanes] * x
        for k in range(CONV_K - 1):
            start = TAIL_ROWS - (CONV_K - 1) + k
            y = y + convw_ref[k:k + 1, lanes] * xbuf_ref[start:start + ts, lanes]
        y = _silu(y)
        if j < 2 * DN_HEADS:
            y = y * lax.rsqrt(jnp.sum(y * y, axis=-1, keepdims=True) + EPS)
        if j < DN_HEADS:
            y = y * (DN_HEAD_DIM ** -0.5)
        act_ref[:, lanes] = y
        xbuf_ref[0:TAIL_ROWS, lanes] = x[ts - TAIL_ROWS:ts, :]

    ba = ba_ref[0]
    beta = _sigmoid(ba)
    g = -jnp.exp(alog_ref[...]) * _softplus(ba + dtb_ref[...])

    n = DN_BLOCK
    r = lax.broadcasted_iota(jnp.int32, (n, n), 0)
    c = lax.broadcasted_iota(jnp.int32, (n, n), 1)
    lower = r >= c
    strict = r > c
    tri_ones = jnp.where(lower, 1.0, 0.0).astype(BF16)

    for ci in range(ts // n):
        rs = slice(ci * n, (ci + 1) * n)
        gc = _dot_exact_lhs(tri_ones, g[rs, :])
        gc_t = gc.T
        beta_c = beta[rs, :]
        for h in range(DN_HEADS):
            q = act_ref[rs, h * LANES:(h + 1) * LANES]
            k = act_ref[rs, width + h * LANES:width + (h + 1) * LANES]
            v = act_ref[rs, 2 * width + h * LANES:2 * width + (h + 1) * LANES]
            b_col = beta_c[:, h:h + 1]
            g_col = gc[:, DN_HEADS + h:DN_HEADS + h + 1]
            g_row = gc_t[DN_HEADS + h:DN_HEADS + h + 1, :]
            g_last = g_col[n - 1:n, :]
            decay = jnp.exp(jnp.where(lower, g_col - g_row, -jnp.inf))
            eg = jnp.exp(g_col)

            k_beta = k * b_col
            kk = _dot_nt(k_beta, k) * decay
            inv = _unit_lower_inverse(jnp.where(strict, kk, 0.0))
            u_val = _dot(inv, v * b_col)
            w_dec = _dot(inv, k_beta * eg)
            qk = _dot_nt(q, k) * decay
            q_dec = q * eg
            k_dec = k * jnp.exp(g_last - g_col)

            state = state_ref[h]
            v_new = u_val - _dot(w_dec, state)
            out = _dot(q_dec, state) + _dot(qk, v_new)
            state_ref[h] = state * jnp.exp(g_last) + _dot(k_dec.T, v_new)

            o = _rms_norm(out, og_ref[...])
            z = z_ref[0, rs, h * LANES:(h + 1) * LANES].astype(F32)
            o_ref[0, rs, h * LANES:(h + 1) * LANES] = (o * _silu(z)).astype(o_ref.dtype)


def _deltanet(qkv, bz, ba, conv_w, a_log_row, dt_bias_row, o_norm_g, *, tile):
    batch, seq, qkv_w = qkv.shape
    width = bz.shape[-1]
    blk = lambda b, t: (b, t, 0)
    const2 = lambda b, t: (0, 0)
    return pl.pallas_call(
        _deltanet_kernel,
        grid=(batch, seq // tile),
        in_specs=[pl.BlockSpec((1, tile, qkv_w), blk),
                  pl.BlockSpec((1, tile, width), blk),
                  pl.BlockSpec((1, tile, LANES), blk),
                  pl.BlockSpec(conv_w.shape, const2),
                  pl.BlockSpec((1, LANES), const2),
                  pl.BlockSpec((1, LANES), const2),
                  pl.BlockSpec((1, DN_HEAD_DIM), const2)],
        out_specs=pl.BlockSpec((1, tile, width), blk),
        out_shape=jax.ShapeDtypeStruct((batch, seq, width), BF16),
        scratch_shapes=[pltpu.VMEM((TAIL_ROWS + tile, qkv_w), F32),
                        pltpu.VMEM((tile, qkv_w), F32),
                        pltpu.VMEM((DN_HEADS, DN_HEAD_DIM, DN_HEAD_DIM), F32)],
        compiler_params=pltpu.CompilerParams(dimension_semantics=("arbitrary", "arbitrary"),
                                             vmem_limit_bytes=VMEM_LIMIT),
        name="deltanet",
    )(qkv, bz, ba, conv_w, a_log_row, dt_bias_row, o_norm_g)


def _outproj_kernel(x_ref, a_ref, b_ref, p_ref, wo_ref, pg_ref, wg_ref, wp_ref, fg_ref, o_ref):
    half = a_ref.shape[-1]
    h = x_ref[...]
    h = h + jnp.dot(a_ref[...], wo_ref[0:half, :], preferred_element_type=F32)
    h = h + jnp.dot(b_ref[...], wo_ref[half:2 * half, :], preferred_element_type=F32)
    hn = _rms_norm(h, pg_ref[...]).astype(BF16)
    gate = _sigmoid(jnp.dot(hn, wg_ref[...], preferred_element_type=F32))
    pe = jnp.dot(p_ref[...].astype(BF16), wp_ref[...], preferred_element_type=F32)
    h = h + gate * pe
    o_ref[...] = _rms_norm(h, fg_ref[...])


def _out_projection(x2, a_out, b_out, p2, w_out, ple_norm_g, ple_gate_w, ple_proj_w, final_norm_g, *, tile):
    tokens, d_model = x2.shape
    row = lambda i: (i, 0)
    const = lambda i: (0, 0)
    return pl.pallas_call(
        _outproj_kernel,
        grid=(tokens // tile,),
        in_specs=[pl.BlockSpec((tile, d_model), row),
                  pl.BlockSpec((tile, a_out.shape[-1]), row),
                  pl.BlockSpec((tile, b_out.shape[-1]), row),
                  pl.BlockSpec((tile, p2.shape[-1]), row),
                  pl.BlockSpec(w_out.shape, const),
                  pl.BlockSpec((1, d_model), const),
                  pl.BlockSpec(ple_gate_w.shape, const),
                  pl.BlockSpec(ple_proj_w.shape, const),
                  pl.BlockSpec((1, d_model), const)],
        out_specs=pl.BlockSpec((tile, d_model), row),
        out_shape=jax.ShapeDtypeStruct((tokens, d_model), F32),
        compiler_params=pltpu.CompilerParams(dimension_semantics=("arbitrary",),
                                             vmem_limit_bytes=VMEM_LIMIT),
        name="out_projection",
    )(x2, a_out, b_out, p2, w_out, ple_norm_g, ple_gate_w, ple_proj_w, final_norm_g)


def _pick_tile(n, target):
    tile = min(n, target)
    assert n % tile == 0, (n, tile)
    return tile


def _lane_row(vec, offset):
    return jnp.zeros((1, LANES), F32).at[0, offset:offset + vec.shape[0]].set(vec.astype(F32))


def kernel(x, p, norm_g, w_in, sgu_ln_g, sgu_ln_b, sgu_w_s, sgu_b_s, dn_conv_w, dn_a_log, dn_dt_bias,
           dn_o_norm_g, w_out, ple_norm_g, ple_gate_w, ple_proj_w, final_norm_g):
    batch, seq, d_model = x.shape
    depth = p.shape[0]
    tokens = batch * seq
    sgu_w = sgu_ln_g.shape[-1]
    dn_w = DN_HEADS * DN_HEAD_DIM
    main_cols = 3 * sgu_w + 4 * dn_w
    assert w_in.shape[-1] == main_cols + 2 * DN_HEADS
    assert seq % SGU_CHUNK == 0 and seq % DN_BLOCK == 0

    h = x.reshape(tokens, d_model)
    out = None
    for i in range(depth):
        w_main = w_in[i, :, :main_cols].astype(BF16)
        w_ba = jnp.zeros((d_model, LANES), BF16).at[:, :2 * DN_HEADS].set(w_in[i, :, main_cols:].astype(BF16))
        au, av, az, qkv, bz, ba = _in_projection(h, norm_g[i][None, :], w_main, w_ba,
                                                 tile=_pick_tile(tokens, 512))
        a_out = _sgu(au, av, az, sgu_ln_g[i][None, :], sgu_ln_b[i][None, :], sgu_w_s[i], sgu_b_s[i].T,
                     tile=_pick_tile(tokens, 512))
        b_out = _deltanet(qkv.reshape(batch, seq, 3 * dn_w), bz.reshape(batch, seq, dn_w),
                          ba.reshape(batch, seq, LANES), dn_conv_w[i],
                          _lane_row(dn_a_log[i], DN_HEADS), _lane_row(dn_dt_bias[i], DN_HEADS),
                          dn_o_norm_g[i][None, :], tile=_pick_tile(seq, 256))
        gains = final_norm_g[None, :] if i == depth - 1 else jnp.ones((1, d_model), F32)
        assert depth == 1, "the fused tail applies the final norm; deeper trunks need an un-normalised variant"
        out = _out_projection(h, a_out, b_out.reshape(tokens, dn_w), p[i].reshape(tokens, -1),
                              w_out[i].astype(BF16), ple_norm_g[i][None, :], ple_gate_w[i].astype(BF16),
                              ple_proj_w[i].astype(BF16), gains, tile=_pick_tile(tokens, 512))
    return out.reshape(batch, seq, d_model)
```

```python
import functools

import jax
import jax.numpy as jnp
from jax import lax
from jax.experimental import pallas as pl
from jax.experimental.pallas import tpu as pltpu

F32 = jnp.float32
BF16 = jnp.bfloat16

EPS = 1e-6
LANES = 128
SGU_GROUPS = 4
SGU_CHUNK = 128
DN_HEADS = 4
DN_HEAD_DIM = 128
CONV_K = 4
DN_BLOCK = 128
INV_BASE = 16
TAIL_ROWS = 8
VMEM_LIMIT = 48 * 1024 * 1024

_SQRT_HALF = 0.7071067811865476


def _sigmoid(x):
    return 1.0 / (1.0 + jnp.exp(-x))


def _silu(x):
    return x * _sigmoid(x)


def _gelu(x):
    return 0.5 * x * (1.0 + lax.erf(x * _SQRT_HALF))


def _softplus(x):
    return jnp.maximum(x, 0.0) + jnp.log1p(jnp.exp(-jnp.abs(x)))


def _rms_norm(x, g):
    return x * lax.rsqrt(jnp.mean(x * x, axis=-1, keepdims=True) + EPS) * g


def _dot(a, b):
    return jnp.dot(a.astype(BF16), b.astype(BF16), preferred_element_type=F32)


def _dot_nt(a, b):
    return lax.dot_general(a.astype(BF16), b.astype(BF16), (((1,), (1,)), ((), ())),
                           preferred_element_type=F32)


def _dot_exact_lhs(lhs_bf16, x):
    hi = x.astype(BF16)
    r1 = x - hi.astype(F32)
    mid = r1.astype(BF16)
    lo = (r1 - mid.astype(F32)).astype(BF16)
    d = lambda t: jnp.dot(lhs_bf16, t, preferred_element_type=F32)
    return d(hi) + d(mid) + d(lo)


def _inproj_kernel(x_ref, g_ref, w_ref, wba_ref, au_ref, av_ref, az_ref, qkv_ref, bz_ref, ba_ref):
    xn = _rms_norm(x_ref[...], g_ref[...]).astype(BF16)
    col = 0
    for out_ref in (au_ref, av_ref, az_ref, qkv_ref, bz_ref):
        width = out_ref.shape[-1]
        out_ref[...] = jnp.dot(xn, w_ref[:, col:col + width], preferred_element_type=F32).astype(out_ref.dtype)
        col += width
    ba_ref[...] = jnp.dot(xn, wba_ref[...], preferred_element_type=F32)


def _in_projection(x2, norm_g, w_main, w_ba, *, tile):
    tokens, d_model = x2.shape
    sgu_w, dn_w = 512, 512
    widths = (sgu_w, sgu_w, sgu_w, 3 * dn_w, dn_w)
    row = lambda i: (i, 0)
    const = lambda i: (0, 0)
    out_shapes = [jax.ShapeDtypeStruct((tokens, w), BF16) for w in widths]
    out_shapes.append(jax.ShapeDtypeStruct((tokens, LANES), F32))
    out_specs = [pl.BlockSpec((tile, w), row) for w in widths] + [pl.BlockSpec((tile, LANES), row)]
    return pl.pallas_call(
        _inproj_kernel,
        grid=(tokens // tile,),
        in_specs=[pl.BlockSpec((tile, d_model), row),
                  pl.BlockSpec((1, d_model), const),
                  pl.BlockSpec(w_main.shape, const),
                  pl.BlockSpec(w_ba.shape, const)],
        out_specs=out_specs,
        out_shape=out_shapes,
        compiler_params=pltpu.CompilerParams(dimension_semantics=("arbitrary",),
                                             vmem_limit_bytes=VMEM_LIMIT),
        name="in_projection",
    )(x2, norm_g, w_main, w_ba)


def _sgu_kernel(u_ref, v_ref, z_ref, lng_ref, lnb_ref, ws_ref, bst_ref, o_ref):
    rows = u_ref.shape[0]
    r = lax.broadcasted_iota(jnp.int32, (SGU_CHUNK, SGU_CHUNK), 0)
    c = lax.broadcasted_iota(jnp.int32, (SGU_CHUNK, SGU_CHUNK), 1)
    causal = r >= c
    for g in range(SGU_GROUPS):
        lanes = slice(g * LANES, (g + 1) * LANES)
        w = jnp.where(causal, ws_ref[g], 0.0).astype(BF16)
        bias = bst_ref[:, g:g + 1]
        ln_g = lng_ref[:, lanes]
        ln_b = lnb_ref[:, lanes]
        for ci in range(rows // SGU_CHUNK):
            rs = slice(ci * SGU_CHUNK, (ci + 1) * SGU_CHUNK)
            v = _gelu(v_ref[rs, lanes].astype(F32))
            mu = jnp.mean(v, axis=-1, keepdims=True)
            vc = v - mu
            y = vc * lax.rsqrt(jnp.mean(vc * vc, axis=-1, keepdims=True) + EPS) * ln_g + ln_b
            s = jnp.dot(w, y.astype(BF16), preferred_element_type=F32) + bias
            u = _gelu(u_ref[rs, lanes].astype(F32))
            z = z_ref[rs, lanes].astype(F32)
            o_ref[rs, lanes] = (u * s * _silu(z)).astype(o_ref.dtype)


def _sgu(au, av, az, ln_g, ln_b, w_s, b_s_t, *, tile):
    tokens, width = au.shape
    row = lambda i: (i, 0)
    const2 = lambda i: (0, 0)
    return pl.pallas_call(
        _sgu_kernel,
        grid=(tokens // tile,),
        in_specs=[pl.BlockSpec((tile, width), row)] * 3 + [
            pl.BlockSpec((1, width), const2),
            pl.BlockSpec((1, width), const2),
            pl.BlockSpec(w_s.shape, lambda i: (0, 0, 0)),
            pl.BlockSpec(b_s_t.shape, const2)],
        out_specs=pl.BlockSpec((tile, width), row),
        out_shape=jax.ShapeDtypeStruct((tokens, width), BF16),
        compiler_params=pltpu.CompilerParams(dimension_semantics=("arbitrary",),
                                             vmem_limit_bytes=VMEM_LIMIT),
        name="sgu",
    )(au, av, az, ln_g, ln_b, w_s, b_s_t)


def _bmm(a, b):
    return lax.dot_general(a.astype(BF16), b.astype(BF16), (((2,), (1,)), ((0,), (0,))),
                           preferred_element_type=F32)


def _bmm_nt(a, b):
    return lax.dot_general(a.astype(BF16), b.astype(BF16), (((2,), (2,)), ((0,), (0,))),
                           preferred_element_type=F32)


def _unit_lower_inverse(strict_lower):
    n = strict_lower.shape[-1]
    r = lax.broadcasted_iota(jnp.int32, (n, n), 0)
    c = lax.broadcasted_iota(jnp.int32, (n, n), 1)
    same_block = lambda size: (r // size) == (c // size)
    m = -jnp.where(same_block(INV_BASE)[None], strict_lower, 0.0)
    inv = jnp.where(r == c, 1.0, 0.0)[None] + m
    power = m
    span = 2
    while span < INV_BASE:
        power = _bmm(power, power)
        inv = inv + _bmm(inv, power)
        span *= 2
    size = INV_BASE
    while size < n:
        coupling = jnp.where((same_block(2 * size) & ~same_block(size))[None], strict_lower, 0.0)
        inv = inv - _bmm(_bmm(inv, coupling), inv)
        size *= 2
    return inv


def _deltanet_kernel(qkv_ref, z_ref, ba_ref, convw_ref, alog_ref, dtb_ref, og_ref, o_ref,
                     xbuf_ref, act_ref, state_ref):
    t = pl.program_id(1)
    ts = qkv_ref.shape[1]
    width = DN_HEADS * DN_HEAD_DIM

    @pl.when(t == 0)
    def _():
        xbuf_ref[0:TAIL_ROWS, :] = jnp.zeros((TAIL_ROWS, xbuf_ref.shape[1]), F32)
        state_ref[...] = jnp.zeros(state_ref.shape, F32)

    for j in range(3 * DN_HEADS):
        lanes = slice(j * LANES, (j + 1) * LANES)
        x = qkv_ref[0, :, lanes].astype(F32)
        xbuf_ref[TAIL_ROWS:TAIL_ROWS + ts, lanes] = x
        y = convw_ref[CONV_K - 1:CONV_K, lanes] * x
        for k in range(CONV_K - 1):
            start = TAIL_ROWS - (CONV_K - 1) + k
            y = y + convw_ref[k:k + 1, lanes] * xbuf_ref[pl.ds(start, ts, stride=1), lanes]
        y = _silu(y)
        if j < 2 * DN_HEADS:
            y = y * lax.rsqrt(jnp.sum(y * y, axis=-1, keepdims=True) + EPS)
        if j < DN_HEADS:
            y = y * (DN_HEAD_DIM ** -0.5)
        act_ref[:, lanes] = y
        xbuf_ref[0:TAIL_ROWS, lanes] = x[ts - TAIL_ROWS:ts, :]

    ba = ba_ref[0]
    beta = _sigmoid(ba)
    g = -jnp.exp(alog_ref[...]) * _softplus(ba + dtb_ref[...])

    n = DN_BLOCK
    r = lax.broadcasted_iota(jnp.int32, (n, n), 0)
    c = lax.broadcasted_iota(jnp.int32, (n, n), 1)
    lower = r >= c
    strict = r > c
    tri_ones = jnp.where(lower, 1.0, 0.0).astype(BF16)

    nblk = ts // n
    gcs = [_dot_exact_lhs(tri_ones, g[ci * n:(ci + 1) * n, :]) for ci in range(nblk)]
    gcs_t = [gc.T for gc in gcs]
    units = [(ci, h) for ci in range(nblk) for h in range(DN_HEADS)]
    stack = lambda f: jnp.stack([f(ci, h) for ci, h in units])
    act = lambda ci, col: act_ref[ci * n:(ci + 1) * n, col * LANES:(col + 1) * LANES]
    q = stack(lambda ci, h: act(ci, h))
    k = stack(lambda ci, h: act(ci, DN_HEADS + h))
    v = stack(lambda ci, h: act(ci, 2 * DN_HEADS + h))
    b_col = stack(lambda ci, h: beta[ci * n:(ci + 1) * n, h:h + 1])
    g_col = stack(lambda ci, h: gcs[ci][:, DN_HEADS + h:DN_HEADS + h + 1])
    g_row = stack(lambda ci, h: gcs_t[ci][DN_HEADS + h:DN_HEADS + h + 1, :])
    g_last = g_col[:, n - 1:n, :]
    decay = jnp.exp(jnp.where(lower[None], g_col - g_row, -jnp.inf))
    eg = jnp.exp(g_col)

    k_beta = k * b_col
    kk = _bmm_nt(k_beta, k) * decay
    inv = _unit_lower_inverse(jnp.where(strict[None], kk, 0.0))
    u_val = _bmm(inv, v * b_col)
    w_dec = _bmm(inv, k_beta * eg)
    qk = _bmm_nt(q, k) * decay
    wq_dec = jnp.concatenate([w_dec, q * eg], axis=1)
    k_dec = k * jnp.exp(g_last - g_col)
    k_dec_t = jnp.stack([k_dec[u].T for u in range(len(units))])
    state_decay = jnp.exp(g_last)

    state = state_ref[...]
    for ci in range(nblk):
        us = slice(ci * DN_HEADS, (ci + 1) * DN_HEADS)
        pre = _bmm(wq_dec[us], state)
        v_new = u_val[us] - pre[:, 0:n, :]
        out = pre[:, n:2 * n, :] + _bmm(qk[us], v_new)
        state = state * state_decay[us] + _bmm(k_dec_t[us], v_new)
        for h in range(DN_HEADS):
            rs = slice(ci * n, (ci + 1) * n)
            o = _rms_norm(out[h], og_ref[...])
            z = z_ref[0, rs, h * LANES:(h + 1) * LANES].astype(F32)
            o_ref[0, rs, h * LANES:(h + 1) * LANES] = (o * _silu(z)).astype(o_ref.dtype)
    state_ref[...] = state


def _deltanet(qkv, bz, ba, conv_w, a_log_row, dt_bias_row, o_norm_g, *, tile):
    batch, seq, qkv_w = qkv.shape
    width = bz.shape[-1]
    blk = lambda b, t: (b, t, 0)
    const2 = lambda b, t: (0, 0)
    return pl.pallas_call(
        _deltanet_kernel,
        grid=(batch, seq // tile),
        in_specs=[pl.BlockSpec((1, tile, qkv_w), blk),
                  pl.BlockSpec((1, tile, width), blk),
                  pl.BlockSpec((1, tile, LANES), blk),
                  pl.BlockSpec(conv_w.shape, const2),
                  pl.BlockSpec((1, LANES), const2),
                  pl.BlockSpec((1, LANES), const2),
                  pl.BlockSpec((1, DN_HEAD_DIM), const2)],
        out_specs=pl.BlockSpec((1, tile, width), blk),
        out_shape=jax.ShapeDtypeStruct((batch, seq, width), BF16),
        scratch_shapes=[pltpu.VMEM((TAIL_ROWS + tile, qkv_w), F32),
                        pltpu.VMEM((tile, qkv_w), F32),
                        pltpu.VMEM((DN_HEADS, DN_HEAD_DIM, DN_HEAD_DIM), F32)],
        compiler_params=pltpu.CompilerParams(dimension_semantics=("arbitrary", "arbitrary"),
                                             vmem_limit_bytes=VMEM_LIMIT),
        name="deltanet",
    )(qkv, bz, ba, conv_w, a_log_row, dt_bias_row, o_norm_g)


def _outproj_kernel(x_ref, a_ref, b_ref, p_ref, wo_ref, pg_ref, wg_ref, wp_ref, fg_ref, o_ref):
    half = a_ref.shape[-1]
    h = x_ref[...]
    h = h + jnp.dot(a_ref[...], wo_ref[0:half, :], preferred_element_type=F32)
    h = h + jnp.dot(b_ref[...], wo_ref[half:2 * half, :], preferred_element_type=F32)
    hn = _rms_norm(h, pg_ref[...]).astype(BF16)
    gate = _sigmoid(jnp.dot(hn, wg_ref[...], preferred_element_type=F32))
    pe = jnp.dot(p_ref[...].astype(BF16), wp_ref[...], preferred_element_type=F32)
    h = h + gate * pe
    o_ref[...] = _rms_norm(h, fg_ref[...])


def _out_projection(x2, a_out, b_out, p2, w_out, ple_norm_g, ple_gate_w, ple_proj_w, final_norm_g, *, tile):
    tokens, d_model = x2.shape
    row = lambda i: (i, 0)
    const = lambda i: (0, 0)
    return pl.pallas_call(
        _outproj_kernel,
        grid=(tokens // tile,),
        in_specs=[pl.BlockSpec((tile, d_model), row),
                  pl.BlockSpec((tile, a_out.shape[-1]), row),
                  pl.BlockSpec((tile, b_out.shape[-1]), row),
                  pl.BlockSpec((tile, p2.shape[-1]), row),
                  pl.BlockSpec(w_out.shape, const),
                  pl.BlockSpec((1, d_model), const),
                  pl.BlockSpec(ple_gate_w.shape, const),
                  pl.BlockSpec(ple_proj_w.shape, const),
                  pl.BlockSpec((1, d_model), const)],
        out_specs=pl.BlockSpec((tile, d_model), row),
        out_shape=jax.ShapeDtypeStruct((tokens, d_model), F32),
        compiler_params=pltpu.CompilerParams(dimension_semantics=("arbitrary",),
                                             vmem_limit_bytes=VMEM_LIMIT),
        name="out_projection",
    )(x2, a_out, b_out, p2, w_out, ple_norm_g, ple_gate_w, ple_proj_w, final_norm_g)


def _pick_tile(n, target):
    tile = min(n, target)
    assert n % tile == 0, (n, tile)
    return tile


def _lane_row(vec, offset):
    return jnp.zeros((1, LANES), F32).at[0, offset:offset + vec.shape[0]].set(vec.astype(F32))


def kernel(x, p, norm_g, w_in, sgu_ln_g, sgu_ln_b, sgu_w_s, sgu_b_s, dn_conv_w, dn_a_log, dn_dt_bias,
           dn_o_norm_g, w_out, ple_norm_g, ple_gate_w, ple_proj_w, final_norm_g):
    batch, seq, d_model = x.shape
    depth = p.shape[0]
    tokens = batch * seq
    sgu_w = sgu_ln_g.shape[-1]
    dn_w = DN_HEADS * DN_HEAD_DIM
    main_cols = 3 * sgu_w + 4 * dn_w
    assert w_in.shape[-1] == main_cols + 2 * DN_HEADS
    assert seq % SGU_CHUNK == 0 and seq % DN_BLOCK == 0

    h = x.reshape(tokens, d_model)
    out = None
    for i in range(depth):
        w_main = w_in[i, :, :main_cols].astype(BF16)
        w_ba = jnp.zeros((d_model, LANES), BF16).at[:, :2 * DN_HEADS].set(w_in[i, :, main_cols:].astype(BF16))
        au, av, az, qkv, bz, ba = _in_projection(h, norm_g[i][None, :], w_main, w_ba,
                                                 tile=_pick_tile(tokens, 512))
        a_out = _sgu(au, av, az, sgu_ln_g[i][None, :], sgu_ln_b[i][None, :], sgu_w_s[i], sgu_b_s[i].T,
                     tile=_pick_tile(tokens, 512))
        b_out = _deltanet(qkv.reshape(batch, seq, 3 * dn_w), bz.reshape(batch, seq, dn_w),
                          ba.reshape(batch, seq, LANES), dn_conv_w[i],
                          _lane_row(dn_a_log[i], DN_HEADS), _lane_row(dn_dt_bias[i], DN_HEADS),
                          dn_o_norm_g[i][None, :], tile=_pick_tile(seq, 256))
        gains = final_norm_g[None, :] if i == depth - 1 else jnp.ones((1, d_model), F32)
        assert depth == 1, "the fused tail applies the final norm; deeper trunks need an un-normalised variant"
        out = _out_projection(h, a_out, b_out.reshape(tokens, dn_w), p[i].reshape(tokens, -1),
                              w_out[i].astype(BF16), ple_norm_g[i][None, :], ple_gate_w[i].astype(BF16),
                              ple_proj_w[i].astype(BF16), gains, tile=_pick_tile(tokens, 512))
    return out.reshape(batch, seq, d_model)
```

```python
import functools

import jax
import jax.numpy as jnp
from jax import lax
from jax.experimental import pallas as pl
from jax.experimental.pallas import tpu as pltpu

F32 = jnp.float32
BF16 = jnp.bfloat16

EPS = 1e-6
LANES = 128
SGU_GROUPS = 4
SGU_CHUNK = 128
DN_HEADS = 4
DN_HEAD_DIM = 128
CONV_K = 4
DN_BLOCK = 128
INV_BASE = 16
TAIL_ROWS = 8
VMEM_LIMIT = 48 * 1024 * 1024
PROJ_TILE = 512
DN_TILE = 512

_SQRT_HALF = 0.7071067811865476


def _sigmoid(x):
    return 1.0 / (1.0 + jnp.exp(-x))


def _silu(x):
    return x * _sigmoid(x)


def _gelu(x):
    return 0.5 * x * (1.0 + lax.erf(x * _SQRT_HALF))


def _softplus(x):
    return jnp.maximum(x, 0.0) + jnp.log1p(jnp.exp(-jnp.abs(x)))


def _rms_norm(x, g):
    return x * lax.rsqrt(jnp.mean(x * x, axis=-1, keepdims=True) + EPS) * g


def _dot_exact_lhs(lhs_bf16, x):
    hi = x.astype(BF16)
    r1 = x - hi.astype(F32)
    mid = r1.astype(BF16)
    lo = (r1 - mid.astype(F32)).astype(BF16)
    d = lambda t: jnp.dot(lhs_bf16, t, preferred_element_type=F32)
    return d(hi) + d(mid) + d(lo)


def _bmm(a, b):
    return lax.dot_general(a.astype(BF16), b.astype(BF16), (((2,), (1,)), ((0,), (0,))),
                           preferred_element_type=F32)


def _bmm_nt(a, b):
    return lax.dot_general(a.astype(BF16), b.astype(BF16), (((2,), (2,)), ((0,), (0,))),
                           preferred_element_type=F32)


def _sgu_tile(u_all, v_all, z_all, lng_ref, lnb_ref, ws_ref, bst_ref, o_ref):
    rows = u_all.shape[0]
    r = lax.broadcasted_iota(jnp.int32, (SGU_CHUNK, SGU_CHUNK), 0)
    c = lax.broadcasted_iota(jnp.int32, (SGU_CHUNK, SGU_CHUNK), 1)
    causal = r >= c
    for g in range(SGU_GROUPS):
        lanes = slice(g * LANES, (g + 1) * LANES)
        w = jnp.where(causal, ws_ref[g], 0.0).astype(BF16)
        bias = bst_ref[:, g:g + 1]
        ln_g = lng_ref[:, lanes]
        ln_b = lnb_ref[:, lanes]
        for ci in range(rows // SGU_CHUNK):
            rs = slice(ci * SGU_CHUNK, (ci + 1) * SGU_CHUNK)
            v = _gelu(v_all[rs, lanes])
            mu = jnp.mean(v, axis=-1, keepdims=True)
            vc = v - mu
            y = vc * lax.rsqrt(jnp.mean(vc * vc, axis=-1, keepdims=True) + EPS) * ln_g + ln_b
            s = jnp.dot(w, y.astype(BF16), preferred_element_type=F32) + bias
            o_ref[rs, lanes] = (_gelu(u_all[rs, lanes]) * s * _silu(z_all[rs, lanes])).astype(o_ref.dtype)


def _conv_tile(qkv, convw_ref, xbuf_ref, ybuf_ref, o_ref):
    ts = qkv.shape[0]
    half = ts // 2
    for j in range(3 * DN_HEADS):
        lanes = slice(j * LANES, (j + 1) * LANES)
        x = qkv[:, lanes]
        xbuf_ref[j, TAIL_ROWS:TAIL_ROWS + ts, :] = x
        for parity in range(2):
            y = None
            for k in range(CONV_K):
                start = TAIL_ROWS + parity - (CONV_K - 1) + k
                term = convw_ref[k:k + 1, lanes] * xbuf_ref[j, pl.ds(start, half, stride=2), :]
                y = term if y is None else y + term
            y = _silu(y)
            if j < 2 * DN_HEADS:
                y = y * lax.rsqrt(jnp.sum(y * y, axis=-1, keepdims=True) + EPS)
            if j < DN_HEADS:
                y = y * (DN_HEAD_DIM ** -0.5)
            ybuf_ref[j, pl.ds(parity, half, stride=2), :] = y
        o_ref[:, lanes] = ybuf_ref[j].astype(o_ref.dtype)
        xbuf_ref[j, 0:TAIL_ROWS, :] = x[ts - TAIL_ROWS:ts, :]


def _inproj_kernel(tiles_per_seq, x_ref, g_ref, w_ref, wba_ref, lng_ref, lnb_ref, ws_ref, bst_ref, convw_ref,
                   a_ref, qkv_ref, bz_ref, ba_ref, xbuf_ref, ybuf_ref):
    sgu_w = a_ref.shape[-1]
    qkv_w = qkv_ref.shape[-1]

    @pl.when(pl.program_id(0) % tiles_per_seq == 0)
    def _():
        xbuf_ref[:, 0:TAIL_ROWS, :] = jnp.zeros((xbuf_ref.shape[0], TAIL_ROWS, LANES), F32)

    xn = _rms_norm(x_ref[...], g_ref[...]).astype(BF16)
    proj = lambda lo, hi: jnp.dot(xn, w_ref[:, lo:hi], preferred_element_type=F32)
    _conv_tile(proj(3 * sgu_w, 3 * sgu_w + qkv_w), convw_ref, xbuf_ref, ybuf_ref, qkv_ref)
    _sgu_tile(proj(0, sgu_w), proj(sgu_w, 2 * sgu_w), proj(2 * sgu_w, 3 * sgu_w),
              lng_ref, lnb_ref, ws_ref, bst_ref, a_ref)
    bz_ref[...] = proj(3 * sgu_w + qkv_w, 3 * sgu_w + qkv_w + bz_ref.shape[-1]).astype(bz_ref.dtype)
    ba_ref[...] = jnp.dot(xn, wba_ref[...], preferred_element_type=F32)


def _in_projection(x2, norm_g, w_main, w_ba, ln_g, ln_b, w_s, b_s_t, conv_w, *, tile, seq):
    tokens, d_model = x2.shape
    sgu_w = ln_g.shape[-1]
    qkv_w = conv_w.shape[-1]
    dn_w = qkv_w // 3
    assert seq % tile == 0 and tile % SGU_CHUNK == 0
    row = lambda i: (i, 0)
    const = lambda i: (0, 0)
    widths = (sgu_w, qkv_w, dn_w)
    out_shapes = [jax.ShapeDtypeStruct((tokens, w), BF16) for w in widths]
    out_shapes.append(jax.ShapeDtypeStruct((tokens, LANES), F32))
    out_specs = [pl.BlockSpec((tile, w), row) for w in widths] + [pl.BlockSpec((tile, LANES), row)]
    return pl.pallas_call(
        functools.partial(_inproj_kernel, seq // tile),
        grid=(tokens // tile,),
        in_specs=[pl.BlockSpec((tile, d_model), row),
                  pl.BlockSpec((1, d_model), const),
                  pl.BlockSpec(w_main.shape, const),
                  pl.BlockSpec(w_ba.shape, const),
                  pl.BlockSpec((1, sgu_w), const),
                  pl.BlockSpec((1, sgu_w), const),
                  pl.BlockSpec(w_s.shape, lambda i: (0, 0, 0)),
                  pl.BlockSpec(b_s_t.shape, const),
                  pl.BlockSpec(conv_w.shape, const)],
        out_specs=out_specs,
        out_shape=out_shapes,
        scratch_shapes=[pltpu.VMEM((qkv_w // LANES, TAIL_ROWS + tile, LANES), F32),
                        pltpu.VMEM((qkv_w // LANES, tile, LANES), F32)],
        compiler_params=pltpu.CompilerParams(dimension_semantics=("arbitrary",),
                                             vmem_limit_bytes=VMEM_LIMIT),
        name="in_projection",
    )(x2, norm_g, w_main, w_ba, ln_g, ln_b, w_s, b_s_t, conv_w)


def _unit_lower_inverse(strict_lower):
    n = strict_lower.shape[-1]
    r = lax.broadcasted_iota(jnp.int32, (n, n), 0)
    c = lax.broadcasted_iota(jnp.int32, (n, n), 1)
    same_block = lambda size: (r // size) == (c // size)
    m = -jnp.where(same_block(INV_BASE)[None], strict_lower, 0.0)
    inv = jnp.where(r == c, 1.0, 0.0)[None] + m
    power = m
    span = 2
    while span < INV_BASE:
        power = _bmm(power, power)
        inv = inv + _bmm(inv, power)
        span *= 2
    size = INV_BASE
    while size < n:
        coupling = jnp.where((same_block(2 * size) & ~same_block(size))[None], strict_lower, 0.0)
        inv = inv - _bmm(_bmm(inv, coupling), inv)
        size *= 2
    return inv


def _deltanet_kernel(qkv_ref, z_ref, ba_ref, alog_ref, dtb_ref, og_ref, o_ref, state_ref):
    ts = qkv_ref.shape[1]
    n = DN_BLOCK

    @pl.when(pl.program_id(1) == 0)
    def _():
        state_ref[...] = jnp.zeros(state_ref.shape, F32)

    ba = ba_ref[0]
    beta = _sigmoid(ba)
    g = -jnp.exp(alog_ref[...]) * _softplus(ba + dtb_ref[...])

    r = lax.broadcasted_iota(jnp.int32, (n, n), 0)
    c = lax.broadcasted_iota(jnp.int32, (n, n), 1)
    lower = r >= c
    strict = r > c
    tri_ones = jnp.where(lower, 1.0, 0.0).astype(BF16)

    nblk = ts // n
    gcs = [_dot_exact_lhs(tri_ones, g[ci * n:(ci + 1) * n, :]) for ci in range(nblk)]
    gcs_t = [gc.T for gc in gcs]
    units = [(ci, h) for ci in range(nblk) for h in range(DN_HEADS)]
    stack = lambda f: jnp.stack([f(ci, h) for ci, h in units])
    act = lambda ci, col: qkv_ref[0, ci * n:(ci + 1) * n, col * LANES:(col + 1) * LANES].astype(F32)
    q = stack(lambda ci, h: act(ci, h))
    k = stack(lambda ci, h: act(ci, DN_HEADS + h))
    v = stack(lambda ci, h: act(ci, 2 * DN_HEADS + h))
    b_col = stack(lambda ci, h: beta[ci * n:(ci + 1) * n, h:h + 1])
    g_col = stack(lambda ci, h: gcs[ci][:, DN_HEADS + h:DN_HEADS + h + 1])
    g_row = stack(lambda ci, h: gcs_t[ci][DN_HEADS + h:DN_HEADS + h + 1, :])
    g_last = g_col[:, n - 1:n, :]
    decay = jnp.exp(jnp.where(lower[None], g_col - g_row, -jnp.inf))
    eg = jnp.exp(g_col)

    k_beta = k * b_col
    kk = _bmm_nt(k_beta, k) * decay
    inv = _unit_lower_inverse(jnp.where(strict[None], kk, 0.0))
    u_val = _bmm(inv, v * b_col)
    w_dec = _bmm(inv, k_beta * eg)
    qk = _bmm_nt(q, k) * decay
    wq_dec = jnp.concatenate([w_dec, q * eg], axis=1)
    k_dec = k * jnp.exp(g_last - g_col)
    k_dec_t = jnp.stack([k_dec[u].T for u in range(len(units))])
    state_decay = jnp.exp(g_last)

    state = state_ref[...]
    for ci in range(nblk):
        us = slice(ci * DN_HEADS, (ci + 1) * DN_HEADS)
        pre = _bmm(wq_dec[us], state)
        v_new = u_val[us] - pre[:, 0:n, :]
        out = pre[:, n:2 * n, :] + _bmm(qk[us], v_new)
        state = state * state_decay[us] + _bmm(k_dec_t[us], v_new)
        for h in range(DN_HEADS):
            rs = slice(ci * n, (ci + 1) * n)
            o = _rms_norm(out[h], og_ref[...])
            z = z_ref[0, rs, h * LANES:(h + 1) * LANES].astype(F32)
            o_ref[0, rs, h * LANES:(h + 1) * LANES] = (o * _silu(z)).astype(o_ref.dtype)
    state_ref[...] = state


def _deltanet(qkv, bz, ba, a_log_row, dt_bias_row, o_norm_g, *, tile):
    batch, seq, qkv_w = qkv.shape
    width = bz.shape[-1]
    assert seq % tile == 0 and tile % DN_BLOCK == 0
    blk = lambda b, t: (b, t, 0)
    const2 = lambda b, t: (0, 0)
    return pl.pallas_call(
        _deltanet_kernel,
        grid=(batch, seq // tile),
        in_specs=[pl.BlockSpec((1, tile, qkv_w), blk),
                  pl.BlockSpec((1, tile, width), blk),
                  pl.BlockSpec((1, tile, LANES), blk),
                  pl.BlockSpec((1, LANES), const2),
                  pl.BlockSpec((1, LANES), const2),
                  pl.BlockSpec((1, DN_HEAD_DIM), const2)],
        out_specs=pl.BlockSpec((1, tile, width), blk),
        out_shape=jax.ShapeDtypeStruct((batch, seq, width), BF16),
        scratch_shapes=[pltpu.VMEM((DN_HEADS, DN_HEAD_DIM, DN_HEAD_DIM), F32)],
        compiler_params=pltpu.CompilerParams(dimension_semantics=("arbitrary", "arbitrary"),
                                             vmem_limit_bytes=VMEM_LIMIT),
        name="deltanet",
    )(qkv, bz, ba, a_log_row, dt_bias_row, o_norm_g)


def _outproj_kernel(x_ref, a_ref, b_ref, p_ref, wo_ref, pg_ref, wg_ref, wp_ref, fg_ref, o_ref):
    half = a_ref.shape[-1]
    h = x_ref[...]
    h = h + jnp.dot(a_ref[...], wo_ref[0:half, :], preferred_element_type=F32)
    h = h + jnp.dot(b_ref[...], wo_ref[half:2 * half, :], preferred_element_type=F32)
    hn = _rms_norm(h, pg_ref[...]).astype(BF16)
    gate = _sigmoid(jnp.dot(hn, wg_ref[...], preferred_element_type=F32))
    pe = jnp.dot(p_ref[...].astype(BF16), wp_ref[...], preferred_element_type=F32)
    h = h + gate * pe
    o_ref[...] = _rms_norm(h, fg_ref[...])


def _out_projection(x2, a_out, b_out, p2, w_out, ple_norm_g, ple_gate_w, ple_proj_w, final_norm_g, *, tile):
    tokens, d_model = x2.shape
    row = lambda i: (i, 0)
    const = lambda i: (0, 0)
    return pl.pallas_call(
        _outproj_kernel,
        grid=(tokens // tile,),
        in_specs=[pl.BlockSpec((tile, d_model), row),
                  pl.BlockSpec((tile, a_out.shape[-1]), row),
                  pl.BlockSpec((tile, b_out.shape[-1]), row),
                  pl.BlockSpec((tile, p2.shape[-1]), row),
                  pl.BlockSpec(w_out.shape, const),
                  pl.BlockSpec((1, d_model), const),
                  pl.BlockSpec(ple_gate_w.shape, const),
                  pl.BlockSpec(ple_proj_w.shape, const),
                  pl.BlockSpec((1, d_model), const)],
        out_specs=pl.BlockSpec((tile, d_model), row),
        out_shape=jax.ShapeDtypeStruct((tokens, d_model), F32),
        compiler_params=pltpu.CompilerParams(dimension_semantics=("arbitrary",),
                                             vmem_limit_bytes=VMEM_LIMIT),
        name="out_projection",
    )(x2, a_out, b_out, p2, w_out, ple_norm_g, ple_gate_w, ple_proj_w, final_norm_g)


def _lane_row(vec, offset):
    return jnp.zeros((1, LANES), F32).at[0, offset:offset + vec.shape[0]].set(vec.astype(F32))


def kernel(x, p, norm_g, w_in, sgu_ln_g, sgu_ln_b, sgu_w_s, sgu_b_s, dn_conv_w, dn_a_log, dn_dt_bias,
           dn_o_norm_g, w_out, ple_norm_g, ple_gate_w, ple_proj_w, final_norm_g):
    batch, seq, d_model = x.shape
    assert p.shape[0] == 1, "single-layer trunk: the last call also applies the final norm"
    tokens = batch * seq
    sgu_w = sgu_ln_g.shape[-1]
    dn_w = DN_HEADS * DN_HEAD_DIM
    main_cols = 3 * sgu_w + 4 * dn_w
    assert w_in.shape[-1] == main_cols + 2 * DN_HEADS
    proj_tile = min(PROJ_TILE, seq)
    dn_tile = min(DN_TILE, seq)

    x2 = x.reshape(tokens, d_model)
    w_main = w_in[0, :, :main_cols].astype(BF16)
    w_ba = jnp.zeros((d_model, LANES), BF16).at[:, :2 * DN_HEADS].set(w_in[0, :, main_cols:].astype(BF16))
    a_out, qkv, bz, ba = _in_projection(
        x2, norm_g[0][None, :], w_main, w_ba, sgu_ln_g[0][None, :], sgu_ln_b[0][None, :], sgu_w_s[0],
        sgu_b_s[0].T, dn_conv_w[0], tile=proj_tile, seq=seq)
    b_out = _deltanet(qkv.reshape(batch, seq, 3 * dn_w), bz.reshape(batch, seq, dn_w),
                      ba.reshape(batch, seq, LANES), _lane_row(dn_a_log[0], DN_HEADS),
                      _lane_row(dn_dt_bias[0], DN_HEADS), dn_o_norm_g[0][None, :], tile=dn_tile)
    out = _out_projection(x2, a_out, b_out.reshape(tokens, dn_w), p[0].reshape(tokens, -1),
                          w_out[0].astype(BF16), ple_norm_g[0][None, :], ple_gate_w[0].astype(BF16),
                          ple_proj_w[0].astype(BF16), final_norm_g[None, :], tile=min(PROJ_TILE, tokens))
    return out.reshape(batch, seq, d_model)
```

```python
import functools

import jax
import jax.numpy as jnp
from jax import lax
from jax.experimental import pallas as pl
from jax.experimental.pallas import tpu as pltpu

F32 = jnp.float32
BF16 = jnp.bfloat16

EPS = 1e-6
LANES = 128
SGU_GROUPS = 4
SGU_CHUNK = 128
DN_HEADS = 4
DN_HEAD_DIM = 128
CONV_K = 4
DN_BLOCK = 128
INV_BASE = 16
TAIL_ROWS = 8
VMEM_LIMIT = 56 * 1024 * 1024
PROJ_TILE = 512
OUT_TILE = 1024
DN_TILE = 1024

_SQRT_HALF = 0.7071067811865476


def _sigmoid(x):
    return 1.0 / (1.0 + jnp.exp(-x))


def _silu(x):
    return x * _sigmoid(x)


def _gelu(x):
    return 0.5 * x * (1.0 + lax.erf(x * _SQRT_HALF))


def _softplus(x):
    return jnp.maximum(x, 0.0) + jnp.log1p(jnp.exp(-jnp.abs(x)))


def _rms_norm(x, g):
    return x * lax.rsqrt(jnp.mean(x * x, axis=-1, keepdims=True) + EPS) * g


def _dot_exact_lhs(lhs_bf16, x):
    hi = x.astype(BF16)
    r1 = x - hi.astype(F32)
    mid = r1.astype(BF16)
    lo = (r1 - mid.astype(F32)).astype(BF16)
    d = lambda t: jnp.dot(lhs_bf16, t, preferred_element_type=F32)
    return d(hi) + d(mid) + d(lo)


def _bmm(a, b):
    return lax.dot_general(a.astype(BF16), b.astype(BF16), (((2,), (1,)), ((0,), (0,))),
                           preferred_element_type=F32)


def _bmm_nt(a, b):
    return lax.dot_general(a.astype(BF16), b.astype(BF16), (((2,), (2,)), ((0,), (0,))),
                           preferred_element_type=F32)


def _sgu_tile(u_all, v_all, z_all, lng_ref, lnb_ref, ws_ref, bst_ref, o_ref):
    rows = u_all.shape[0]
    r = lax.broadcasted_iota(jnp.int32, (SGU_CHUNK, SGU_CHUNK), 0)
    c = lax.broadcasted_iota(jnp.int32, (SGU_CHUNK, SGU_CHUNK), 1)
    causal = r >= c
    for g in range(SGU_GROUPS):
        lanes = slice(g * LANES, (g + 1) * LANES)
        w = jnp.where(causal, ws_ref[g], 0.0).astype(BF16)
        bias = bst_ref[:, g:g + 1]
        ln_g = lng_ref[:, lanes]
        ln_b = lnb_ref[:, lanes]
        for ci in range(rows // SGU_CHUNK):
            rs = slice(ci * SGU_CHUNK, (ci + 1) * SGU_CHUNK)
            v = _gelu(v_all[rs, lanes])
            mu = jnp.mean(v, axis=-1, keepdims=True)
            vc = v - mu
            y = vc * lax.rsqrt(jnp.mean(vc * vc, axis=-1, keepdims=True) + EPS) * ln_g + ln_b
            s = jnp.dot(w, y.astype(BF16), preferred_element_type=F32) + bias
            o_ref[rs, lanes] = (_gelu(u_all[rs, lanes]) * s * _silu(z_all[rs, lanes])).astype(o_ref.dtype)


def _conv_tile(qkv, convw_ref, xbuf_ref, ybuf_ref, o_ref):
    ts = qkv.shape[0]
    half = ts // 2
    for j in range(3 * DN_HEADS):
        lanes = slice(j * LANES, (j + 1) * LANES)
        x = qkv[:, lanes]
        xbuf_ref[j, TAIL_ROWS:TAIL_ROWS + ts, :] = x
        for parity in range(2):
            y = None
            for k in range(CONV_K):
                start = TAIL_ROWS + parity - (CONV_K - 1) + k
                term = convw_ref[k:k + 1, lanes] * xbuf_ref[j, pl.ds(start, half, stride=2), :]
                y = term if y is None else y + term
            y = _silu(y)
            if j < 2 * DN_HEADS:
                y = y * lax.rsqrt(jnp.sum(y * y, axis=-1, keepdims=True) + EPS)
            if j < DN_HEADS:
                y = y * (DN_HEAD_DIM ** -0.5)
            ybuf_ref[j, pl.ds(parity, half, stride=2), :] = y
        o_ref[:, lanes] = ybuf_ref[j].astype(o_ref.dtype)
        xbuf_ref[j, 0:TAIL_ROWS, :] = x[ts - TAIL_ROWS:ts, :]


def _inproj_kernel(tiles_per_seq, x_ref, g_ref, w_ref, wba_ref, lng_ref, lnb_ref, ws_ref, bst_ref, convw_ref,
                   a_ref, qkv_ref, bz_ref, ba_ref, xbuf_ref, ybuf_ref):
    sgu_w = a_ref.shape[-1]
    qkv_w = qkv_ref.shape[-1]

    @pl.when(pl.program_id(0) % tiles_per_seq == 0)
    def _():
        xbuf_ref[:, 0:TAIL_ROWS, :] = jnp.zeros((xbuf_ref.shape[0], TAIL_ROWS, LANES), F32)

    xn = _rms_norm(x_ref[...], g_ref[...]).astype(BF16)
    proj = lambda lo, hi: jnp.dot(xn, w_ref[:, lo:hi], preferred_element_type=F32)
    _conv_tile(proj(3 * sgu_w, 3 * sgu_w + qkv_w), convw_ref, xbuf_ref, ybuf_ref, qkv_ref)
    _sgu_tile(proj(0, sgu_w), proj(sgu_w, 2 * sgu_w), proj(2 * sgu_w, 3 * sgu_w),
              lng_ref, lnb_ref, ws_ref, bst_ref, a_ref)
    bz_ref[...] = proj(3 * sgu_w + qkv_w, 3 * sgu_w + qkv_w + bz_ref.shape[-1]).astype(bz_ref.dtype)
    ba_ref[...] = jnp.dot(xn, wba_ref[...], preferred_element_type=F32)


def _in_projection(x2, norm_g, w_main, w_ba, ln_g, ln_b, w_s, b_s_t, conv_w, *, tile, seq):
    tokens, d_model = x2.shape
    sgu_w = ln_g.shape[-1]
    qkv_w = conv_w.shape[-1]
    dn_w = qkv_w // 3
    assert seq % tile == 0 and tile % SGU_CHUNK == 0
    row = lambda i: (i, 0)
    const = lambda i: (0, 0)
    widths = (sgu_w, qkv_w, dn_w)
    out_shapes = [jax.ShapeDtypeStruct((tokens, w), BF16) for w in widths]
    out_shapes.append(jax.ShapeDtypeStruct((tokens, LANES), F32))
    out_specs = [pl.BlockSpec((tile, w), row) for w in widths] + [pl.BlockSpec((tile, LANES), row)]
    return pl.pallas_call(
        functools.partial(_inproj_kernel, seq // tile),
        grid=(tokens // tile,),
        in_specs=[pl.BlockSpec((tile, d_model), row),
                  pl.BlockSpec((1, d_model), const),
                  pl.BlockSpec(w_main.shape, const),
                  pl.BlockSpec(w_ba.shape, const),
                  pl.BlockSpec((1, sgu_w), const),
                  pl.BlockSpec((1, sgu_w), const),
                  pl.BlockSpec(w_s.shape, lambda i: (0, 0, 0)),
                  pl.BlockSpec(b_s_t.shape, const),
                  pl.BlockSpec(conv_w.shape, const)],
        out_specs=out_specs,
        out_shape=out_shapes,
        scratch_shapes=[pltpu.VMEM((qkv_w // LANES, TAIL_ROWS + tile, LANES), F32),
                        pltpu.VMEM((qkv_w // LANES, tile, LANES), F32)],
        compiler_params=pltpu.CompilerParams(dimension_semantics=("arbitrary",),
                                             vmem_limit_bytes=VMEM_LIMIT),
        name="in_projection",
    )(x2, norm_g, w_main, w_ba, ln_g, ln_b, w_s, b_s_t, conv_w)


def _unit_lower_inverse(strict_lower):
    n = strict_lower.shape[-1]
    r = lax.broadcasted_iota(jnp.int32, (n, n), 0)
    c = lax.broadcasted_iota(jnp.int32, (n, n), 1)
    same_block = lambda size: (r // size) == (c // size)
    m = -jnp.where(same_block(INV_BASE)[None], strict_lower, 0.0)
    inv = jnp.where(r == c, 1.0, 0.0)[None] + m
    power = m
    span = 2
    while span < INV_BASE:
        power = _bmm(power, power)
        inv = inv + _bmm(inv, power)
        span *= 2
    size = INV_BASE
    while size < n:
        coupling = jnp.where((same_block(2 * size) & ~same_block(size))[None], strict_lower, 0.0)
        inv = inv - _bmm(_bmm(inv, coupling), inv)
        size *= 2
    return inv


def _deltanet_kernel(qkv_ref, z_ref, ba_ref, alog_ref, dtb_ref, og_ref, o_ref, state_ref):
    ts = qkv_ref.shape[1]
    n = DN_BLOCK

    @pl.when(pl.program_id(1) == 0)
    def _():
        state_ref[...] = jnp.zeros(state_ref.shape, F32)

    ba = ba_ref[0]
    beta = _sigmoid(ba)
    g = -jnp.exp(alog_ref[...]) * _softplus(ba + dtb_ref[...])

    r = lax.broadcasted_iota(jnp.int32, (n, n), 0)
    c = lax.broadcasted_iota(jnp.int32, (n, n), 1)
    lower = r >= c
    strict = r > c
    tri_ones = jnp.where(lower, 1.0, 0.0).astype(BF16)

    nblk = ts // n
    gcs = [_dot_exact_lhs(tri_ones, g[ci * n:(ci + 1) * n, :]) for ci in range(nblk)]
    gcs_t = [gc.T for gc in gcs]
    units = [(ci, h) for ci in range(nblk) for h in range(DN_HEADS)]
    stack = lambda f: jnp.stack([f(ci, h) for ci, h in units])
    act = lambda ci, col: qkv_ref[0, ci * n:(ci + 1) * n, col * LANES:(col + 1) * LANES]
    q_bf = stack(lambda ci, h: act(ci, h))
    k_bf = stack(lambda ci, h: act(ci, DN_HEADS + h))
    q = q_bf.astype(F32)
    k = k_bf.astype(F32)
    v = stack(lambda ci, h: act(ci, 2 * DN_HEADS + h)).astype(F32)
    full = (len(units), n, LANES)
    b_full = jnp.broadcast_to(stack(lambda ci, h: beta[ci * n:(ci + 1) * n, h:h + 1]), full)
    g_full = jnp.broadcast_to(stack(lambda ci, h: gcs[ci][:, DN_HEADS + h:DN_HEADS + h + 1]), full)
    g_row = stack(lambda ci, h: gcs_t[ci][DN_HEADS + h:DN_HEADS + h + 1, :])
    g_last = g_full[:, n - 1:n, :]
    decay = jnp.exp(jnp.where(lower[None], g_full - g_row, -jnp.inf))
    eg = jnp.exp(g_full)

    k_beta = k * b_full
    kk = _bmm_nt(k_beta, k_bf) * decay
    inv = _unit_lower_inverse(jnp.where(strict[None], kk, 0.0))
    wu = _bmm(inv, jnp.concatenate([k_beta * eg, v * b_full], axis=2))
    qk = _bmm_nt(q_bf, k_bf) * decay
    k_dec = k * jnp.exp(g_last - g_full)
    k_dec_t = jnp.stack([k_dec[u].T for u in range(len(units))])
    state_decay = jnp.exp(g_last)

    kd_wu = _bmm(k_dec_t, wu)
    qk_wu = _bmm(qk, wu)
    state_in = jnp.concatenate([-kd_wu[:, :, 0:LANES], q * eg - qk_wu[:, :, 0:LANES]], axis=1)
    state_add = kd_wu[:, :, LANES:2 * LANES]
    out_add = qk_wu[:, :, LANES:2 * LANES]

    state = state_ref[...]
    for ci in range(nblk):
        us = slice(ci * DN_HEADS, (ci + 1) * DN_HEADS)
        prod = _bmm(state_in[us], state)
        out = prod[:, n:2 * n, :] + out_add[us]
        state = state * state_decay[us] + prod[:, 0:n, :] + state_add[us]
        for h in range(DN_HEADS):
            rs = slice(ci * n, (ci + 1) * n)
            o = _rms_norm(out[h], og_ref[...])
            z = z_ref[0, rs, h * LANES:(h + 1) * LANES].astype(F32)
            o_ref[0, rs, h * LANES:(h + 1) * LANES] = (o * _silu(z)).astype(o_ref.dtype)
    state_ref[...] = state


def _deltanet(qkv, bz, ba, a_log_row, dt_bias_row, o_norm_g, *, tile):
    batch, seq, qkv_w = qkv.shape
    width = bz.shape[-1]
    assert seq % tile == 0 and tile % DN_BLOCK == 0
    assert DN_BLOCK == LANES == DN_HEAD_DIM
    blk = lambda b, t: (b, t, 0)
    const2 = lambda b, t: (0, 0)
    return pl.pallas_call(
        _deltanet_kernel,
        grid=(batch, seq // tile),
        in_specs=[pl.BlockSpec((1, tile, qkv_w), blk),
                  pl.BlockSpec((1, tile, width), blk),
                  pl.BlockSpec((1, tile, LANES), blk),
                  pl.BlockSpec((1, LANES), const2),
                  pl.BlockSpec((1, LANES), const2),
                  pl.BlockSpec((1, DN_HEAD_DIM), const2)],
        out_specs=pl.BlockSpec((1, tile, width), blk),
        out_shape=jax.ShapeDtypeStruct((batch, seq, width), BF16),
        scratch_shapes=[pltpu.VMEM((DN_HEADS, DN_HEAD_DIM, DN_HEAD_DIM), F32)],
        compiler_params=pltpu.CompilerParams(dimension_semantics=("arbitrary", "arbitrary"),
                                             vmem_limit_bytes=VMEM_LIMIT),
        name="deltanet",
    )(qkv, bz, ba, a_log_row, dt_bias_row, o_norm_g)


def _outproj_kernel(x_ref, a_ref, b_ref, p_ref, wo_ref, pg_ref, wg_ref, wp_ref, fg_ref, o_ref):
    half = a_ref.shape[-1]
    h = x_ref[...]
    h = h + jnp.dot(a_ref[...], wo_ref[0:half, :], preferred_element_type=F32)
    h = h + jnp.dot(b_ref[...], wo_ref[half:2 * half, :], preferred_element_type=F32)
    hn = _rms_norm(h, pg_ref[...]).astype(BF16)
    gate = _sigmoid(jnp.dot(hn, wg_ref[...], preferred_element_type=F32))
    pe = jnp.dot(p_ref[...].astype(BF16), wp_ref[...], preferred_element_type=F32)
    h = h + gate * pe
    o_ref[...] = _rms_norm(h, fg_ref[...])


def _out_projection(x2, a_out, b_out, p2, w_out, ple_norm_g, ple_gate_w, ple_proj_w, final_norm_g, *, tile):
    tokens, d_model = x2.shape
    row = lambda i: (i, 0)
    const = lambda i: (0, 0)
    return pl.pallas_call(
        _outproj_kernel,
        grid=(tokens // tile,),
        in_specs=[pl.BlockSpec((tile, d_model), row),
                  pl.BlockSpec((tile, a_out.shape[-1]), row),
                  pl.BlockSpec((tile, b_out.shape[-1]), row),
                  pl.BlockSpec((tile, p2.shape[-1]), row),
                  pl.BlockSpec(w_out.shape, const),
                  pl.BlockSpec((1, d_model), const),
                  pl.BlockSpec(ple_gate_w.shape, const),
                  pl.BlockSpec(ple_proj_w.shape, const),
                  pl.BlockSpec((1, d_model), const)],
        out_specs=pl.BlockSpec((tile, d_model), row),
        out_shape=jax.ShapeDtypeStruct((tokens, d_model), F32),
        compiler_params=pltpu.CompilerParams(dimension_semantics=("arbitrary",),
                                             vmem_limit_bytes=VMEM_LIMIT),
        name="out_projection",
    )(x2, a_out, b_out, p2, w_out, ple_norm_g, ple_gate_w, ple_proj_w, final_norm_g)


def _lane_row(vec, offset):
    return jnp.zeros((1, LANES), F32).at[0, offset:offset + vec.shape[0]].set(vec.astype(F32))


def kernel(x, p, norm_g, w_in, sgu_ln_g, sgu_ln_b, sgu_w_s, sgu_b_s, dn_conv_w, dn_a_log, dn_dt_bias,
           dn_o_norm_g, w_out, ple_norm_g, ple_gate_w, ple_proj_w, final_norm_g):
    batch, seq, d_model = x.shape
    assert p.shape[0] == 1, "single-layer trunk: the last call also applies the final norm"
    tokens = batch * seq
    sgu_w = sgu_ln_g.shape[-1]
    dn_w = DN_HEADS * DN_HEAD_DIM
    main_cols = 3 * sgu_w + 4 * dn_w
    assert w_in.shape[-1] == main_cols + 2 * DN_HEADS
    proj_tile = min(PROJ_TILE, seq)
    dn_tile = min(DN_TILE, seq)

    x2 = x.reshape(tokens, d_model)
    w_main = w_in[0, :, :main_cols].astype(BF16)
    w_ba = jnp.zeros((d_model, LANES), BF16).at[:, :2 * DN_HEADS].set(w_in[0, :, main_cols:].astype(BF16))
    a_out, qkv, bz, ba = _in_projection(
        x2, norm_g[0][None, :], w_main, w_ba, sgu_ln_g[0][None, :], sgu_ln_b[0][None, :], sgu_w_s[0],
        sgu_b_s[0].T, dn_conv_w[0], tile=proj_tile, seq=seq)
    b_out = _deltanet(qkv.reshape(batch, seq, 3 * dn_w), bz.reshape(batch, seq, dn_w),
                      ba.reshape(batch, seq, LANES), _lane_row(dn_a_log[0], DN_HEADS),
                      _lane_row(dn_dt_bias[0], DN_HEADS), dn_o_norm_g[0][None, :], tile=dn_tile)
    out = _out_projection(x2, a_out, b_out.reshape(tokens, dn_w), p[0].reshape(tokens, -1),
                          w_out[0].astype(BF16), ple_norm_g[0][None, :], ple_gate_w[0].astype(BF16),
                          ple_proj_w[0].astype(BF16), final_norm_g[None, :], tile=min(OUT_TILE, tokens))
    return out.reshape(batch, seq, d_model)
```

```python
import functools

import jax
import jax.numpy as jnp
from jax import lax
from jax.experimental import pallas as pl
from jax.experimental.pallas import tpu as pltpu

F32 = jnp.float32
BF16 = jnp.bfloat16

EPS = 1e-6
LANES = 128
SGU_GROUPS = 4
SGU_CHUNK = 128
DN_HEADS = 4
DN_HEAD_DIM = 128
CONV_K = 4
DN_BLOCK = 128
INV_BASE = 8
TAIL_ROWS = 8
VMEM_LIMIT = 56 * 1024 * 1024
PROJ_TILE = 512
OUT_TILE = 1024
DN_TILE = 1024

_SQRT_HALF = 0.7071067811865476


def _sigmoid(x):
    return 1.0 / (1.0 + jnp.exp(-x))


def _silu(x):
    return x * _sigmoid(x)


def _gelu(x):
    return 0.5 * x * (1.0 + lax.erf(x * _SQRT_HALF))


def _softplus(x):
    return jnp.maximum(x, 0.0) + jnp.log1p(jnp.exp(-jnp.abs(x)))


def _rms_norm(x, g):
    return x * lax.rsqrt(jnp.mean(x * x, axis=-1, keepdims=True) + EPS) * g


def _dot_exact_lhs(lhs_bf16, x):
    hi = x.astype(BF16)
    r1 = x - hi.astype(F32)
    mid = r1.astype(BF16)
    lo = (r1 - mid.astype(F32)).astype(BF16)
    d = lambda t: jnp.dot(lhs_bf16, t, preferred_element_type=F32)
    return d(hi) + d(mid) + d(lo)


def _bmm(a, b):
    return lax.dot_general(a.astype(BF16), b.astype(BF16), (((2,), (1,)), ((0,), (0,))),
                           preferred_element_type=F32)


def _bmm_nt(a, b):
    return lax.dot_general(a.astype(BF16), b.astype(BF16), (((2,), (2,)), ((0,), (0,))),
                           preferred_element_type=F32)


def _sgu_tile(u_all, v_all, z_all, lng_ref, lnb_ref, ws_ref, bst_ref, o_ref):
    rows = u_all.shape[0]
    r = lax.broadcasted_iota(jnp.int32, (SGU_CHUNK, SGU_CHUNK), 0)
    c = lax.broadcasted_iota(jnp.int32, (SGU_CHUNK, SGU_CHUNK), 1)
    causal = r >= c
    for g in range(SGU_GROUPS):
        lanes = slice(g * LANES, (g + 1) * LANES)
        w = jnp.where(causal, ws_ref[g], 0.0).astype(BF16)
        bias = bst_ref[:, g:g + 1]
        ln_g = lng_ref[:, lanes]
        ln_b = lnb_ref[:, lanes]
        for ci in range(rows // SGU_CHUNK):
            rs = slice(ci * SGU_CHUNK, (ci + 1) * SGU_CHUNK)
            v = _gelu(v_all[rs, lanes])
            mu = jnp.mean(v, axis=-1, keepdims=True)
            vc = v - mu
            y = vc * lax.rsqrt(jnp.mean(vc * vc, axis=-1, keepdims=True) + EPS) * ln_g + ln_b
            s = jnp.dot(w, y.astype(BF16), preferred_element_type=F32) + bias
            o_ref[rs, lanes] = (_gelu(u_all[rs, lanes]) * s * _silu(z_all[rs, lanes])).astype(o_ref.dtype)


def _conv_tile(qkv, convw_ref, xbuf_ref, ybuf_ref, o_ref):
    ts = qkv.shape[0]
    half = ts // 2
    for j in range(3 * DN_HEADS):
        lanes = slice(j * LANES, (j + 1) * LANES)
        x = qkv[:, lanes]
        xbuf_ref[j, TAIL_ROWS:TAIL_ROWS + ts, :] = x
        first = TAIL_ROWS - (CONV_K - 1)
        views = [xbuf_ref[j, pl.ds(first + s, half, stride=2), :] for s in range(CONV_K + 1)]
        for parity in range(2):
            y = None
            for k in range(CONV_K):
                term = convw_ref[k:k + 1, lanes] * views[parity + k]
                y = term if y is None else y + term
            y = _silu(y)
            if j < 2 * DN_HEADS:
                y = y * lax.rsqrt(jnp.sum(y * y, axis=-1, keepdims=True) + EPS)
            if j < DN_HEADS:
                y = y * (DN_HEAD_DIM ** -0.5)
            ybuf_ref[j, pl.ds(parity, half, stride=2), :] = y
        o_ref[:, lanes] = ybuf_ref[j].astype(o_ref.dtype)
        xbuf_ref[j, 0:TAIL_ROWS, :] = x[ts - TAIL_ROWS:ts, :]


def _inproj_kernel(tiles_per_seq, x_ref, g_ref, w_ref, wba_ref, lng_ref, lnb_ref, ws_ref, bst_ref, convw_ref,
                   a_ref, qkv_ref, bz_ref, ba_ref, xbuf_ref, ybuf_ref):
    sgu_w = a_ref.shape[-1]
    qkv_w = qkv_ref.shape[-1]

    @pl.when(pl.program_id(0) % tiles_per_seq == 0)
    def _():
        xbuf_ref[:, 0:TAIL_ROWS, :] = jnp.zeros((xbuf_ref.shape[0], TAIL_ROWS, LANES), F32)

    xn = _rms_norm(x_ref[...], g_ref[...]).astype(BF16)
    proj = lambda lo, hi: jnp.dot(xn, w_ref[:, lo:hi], preferred_element_type=F32)
    _conv_tile(proj(3 * sgu_w, 3 * sgu_w + qkv_w), convw_ref, xbuf_ref, ybuf_ref, qkv_ref)
    _sgu_tile(proj(0, sgu_w), proj(sgu_w, 2 * sgu_w), proj(2 * sgu_w, 3 * sgu_w),
              lng_ref, lnb_ref, ws_ref, bst_ref, a_ref)
    bz_ref[...] = proj(3 * sgu_w + qkv_w, 3 * sgu_w + qkv_w + bz_ref.shape[-1]).astype(bz_ref.dtype)
    ba_ref[...] = jnp.dot(xn, wba_ref[...], preferred_element_type=F32)


def _in_projection(x2, norm_g, w_main, w_ba, ln_g, ln_b, w_s, b_s_t, conv_w, *, tile, seq):
    tokens, d_model = x2.shape
    sgu_w = ln_g.shape[-1]
    qkv_w = conv_w.shape[-1]
    dn_w = qkv_w // 3
    assert seq % tile == 0 and tile % SGU_CHUNK == 0
    row = lambda i: (i, 0)
    const = lambda i: (0, 0)
    widths = (sgu_w, qkv_w, dn_w)
    out_shapes = [jax.ShapeDtypeStruct((tokens, w), BF16) for w in widths]
    out_shapes.append(jax.ShapeDtypeStruct((tokens, LANES), F32))
    out_specs = [pl.BlockSpec((tile, w), row) for w in widths] + [pl.BlockSpec((tile, LANES), row)]
    return pl.pallas_call(
        functools.partial(_inproj_kernel, seq // tile),
        grid=(tokens // tile,),
        in_specs=[pl.BlockSpec((tile, d_model), row),
                  pl.BlockSpec((1, d_model), const),
                  pl.BlockSpec(w_main.shape, const),
                  pl.BlockSpec(w_ba.shape, const),
                  pl.BlockSpec((1, sgu_w), const),
                  pl.BlockSpec((1, sgu_w), const),
                  pl.BlockSpec(w_s.shape, lambda i: (0, 0, 0)),
                  pl.BlockSpec(b_s_t.shape, const),
                  pl.BlockSpec(conv_w.shape, const)],
        out_specs=out_specs,
        out_shape=out_shapes,
        scratch_shapes=[pltpu.VMEM((qkv_w // LANES, TAIL_ROWS + tile, LANES), F32),
                        pltpu.VMEM((qkv_w // LANES, tile, LANES), F32)],
        compiler_params=pltpu.CompilerParams(dimension_semantics=("arbitrary",),
                                             vmem_limit_bytes=VMEM_LIMIT),
        name="in_projection",
    )(x2, norm_g, w_main, w_ba, ln_g, ln_b, w_s, b_s_t, conv_w)


def _unit_lower_inverse(strict_lower):
    n = strict_lower.shape[-1]
    r = lax.broadcasted_iota(jnp.int32, (n, n), 0)
    c = lax.broadcasted_iota(jnp.int32, (n, n), 1)
    same_block = lambda size: (r // size) == (c // size)
    m = -jnp.where(same_block(INV_BASE)[None], strict_lower, 0.0)
    inv = jnp.where(r == c, 1.0, 0.0)[None] + m
    power = m
    span = 2
    while span < INV_BASE:
        power = _bmm(power, power)
        inv = inv + _bmm(inv, power)
        span *= 2
    size = INV_BASE
    while size < n:
        coupling = jnp.where((same_block(2 * size) & ~same_block(size))[None], strict_lower, 0.0)
        inv = inv - _bmm(_bmm(inv, coupling), inv)
        size *= 2
    return inv


def _deltanet_kernel(qkv_ref, z_ref, ba_ref, alog_ref, dtb_ref, og_ref, o_ref, state_ref):
    ts = qkv_ref.shape[1]
    n = DN_BLOCK

    @pl.when(pl.program_id(1) == 0)
    def _():
        state_ref[...] = jnp.zeros(state_ref.shape, F32)

    ba = ba_ref[0]
    beta = _sigmoid(ba)
    g = -jnp.exp(alog_ref[...]) * _softplus(ba + dtb_ref[...])

    r = lax.broadcasted_iota(jnp.int32, (n, n), 0)
    c = lax.broadcasted_iota(jnp.int32, (n, n), 1)
    lower = r >= c
    strict = r > c
    tri_ones = jnp.where(lower, 1.0, 0.0).astype(BF16)

    nblk = ts // n
    gcs = [_dot_exact_lhs(tri_ones, g[ci * n:(ci + 1) * n, :]) for ci in range(nblk)]
    gcs_t = [gc.T for gc in gcs]
    units = [(ci, h) for ci in range(nblk) for h in range(DN_HEADS)]
    stack = lambda f: jnp.stack([f(ci, h) for ci, h in units])
    act = lambda ci, col: qkv_ref[0, ci * n:(ci + 1) * n, col * LANES:(col + 1) * LANES]
    q_bf = stack(lambda ci, h: act(ci, h))
    k_bf = stack(lambda ci, h: act(ci, DN_HEADS + h))
    q = q_bf.astype(F32)
    k = k_bf.astype(F32)
    v = stack(lambda ci, h: act(ci, 2 * DN_HEADS + h)).astype(F32)
    full = (len(units), n, LANES)
    b_full = jnp.broadcast_to(stack(lambda ci, h: beta[ci * n:(ci + 1) * n, h:h + 1]), full)
    g_full = jnp.broadcast_to(stack(lambda ci, h: gcs[ci][:, DN_HEADS + h:DN_HEADS + h + 1]), full)
    g_row = stack(lambda ci, h: gcs_t[ci][DN_HEADS + h:DN_HEADS + h + 1, :])
    g_last = g_full[:, n - 1:n, :]
    decay = jnp.exp(jnp.where(lower[None], g_full - g_row, -jnp.inf))
    eg = jnp.exp(g_full)

    k_beta = k * b_full
    kk = _bmm_nt(k_beta, k_bf) * decay
    inv = _unit_lower_inverse(jnp.where(strict[None], kk, 0.0))
    wu = _bmm(inv, jnp.concatenate([k_beta * eg, v * b_full], axis=2))
    qk = _bmm_nt(q_bf, k_bf) * decay
    k_dec = k * jnp.exp(g_last - g_full)
    k_dec_t = jnp.stack([k_dec[u].T for u in range(len(units))])
    state_decay = jnp.exp(g_last)

    kd_wu = _bmm(k_dec_t, wu)
    qk_wu = _bmm(qk, wu)
    state_in = jnp.concatenate([-kd_wu[:, :, 0:LANES], q * eg - qk_wu[:, :, 0:LANES]], axis=1)
    state_add = kd_wu[:, :, LANES:2 * LANES]
    out_add = qk_wu[:, :, LANES:2 * LANES]

    state = state_ref[...]
    for ci in range(nblk):
        us = slice(ci * DN_HEADS, (ci + 1) * DN_HEADS)
        prod = _bmm(state_in[us], state)
        out = prod[:, n:2 * n, :] + out_add[us]
        state = state * state_decay[us] + prod[:, 0:n, :] + state_add[us]
        for h in range(DN_HEADS):
            rs = slice(ci * n, (ci + 1) * n)
            o = _rms_norm(out[h], og_ref[...])
            z = z_ref[0, rs, h * LANES:(h + 1) * LANES].astype(F32)
            o_ref[0, rs, h * LANES:(h + 1) * LANES] = (o * _silu(z)).astype(o_ref.dtype)
    state_ref[...] = state


def _deltanet(qkv, bz, ba, a_log_row, dt_bias_row, o_norm_g, *, tile):
    batch, seq, qkv_w = qkv.shape
    width = bz.shape[-1]
    assert seq % tile == 0 and tile % DN_BLOCK == 0
    assert DN_BLOCK == LANES == DN_HEAD_DIM
    blk = lambda b, t: (b, t, 0)
    const2 = lambda b, t: (0, 0)
    return pl.pallas_call(
        _deltanet_kernel,
        grid=(batch, seq // tile),
        in_specs=[pl.BlockSpec((1, tile, qkv_w), blk),
                  pl.BlockSpec((1, tile, width), blk),
                  pl.BlockSpec((1, tile, LANES), blk),
                  pl.BlockSpec((1, LANES), const2),
                  pl.BlockSpec((1, LANES), const2),
                  pl.BlockSpec((1, DN_HEAD_DIM), const2)],
        out_specs=pl.BlockSpec((1, tile, width), blk),
        out_shape=jax.ShapeDtypeStruct((batch, seq, width), BF16),
        scratch_shapes=[pltpu.VMEM((DN_HEADS, DN_HEAD_DIM, DN_HEAD_DIM), F32)],
        compiler_params=pltpu.CompilerParams(dimension_semantics=("arbitrary", "arbitrary"),
                                             vmem_limit_bytes=VMEM_LIMIT),
        name="deltanet",
    )(qkv, bz, ba, a_log_row, dt_bias_row, o_norm_g)


def _outproj_kernel(x_ref, a_ref, b_ref, p_ref, wo_ref, pg_ref, wg_ref, wp_ref, fg_ref, o_ref):
    half = a_ref.shape[-1]
    h = x_ref[...]
    h = h + jnp.dot(a_ref[...], wo_ref[0:half, :], preferred_element_type=F32)
    h = h + jnp.dot(b_ref[...], wo_ref[half:2 * half, :], preferred_element_type=F32)
    hn = _rms_norm(h, pg_ref[...]).astype(BF16)
    gate = _sigmoid(jnp.dot(hn, wg_ref[...], preferred_element_type=F32))
    pe = jnp.dot(p_ref[...].astype(BF16), wp_ref[...], preferred_element_type=F32)
    h = h + gate * pe
    o_ref[...] = _rms_norm(h, fg_ref[...])


def _out_projection(x2, a_out, b_out, p2, w_out, ple_norm_g, ple_gate_w, ple_proj_w, final_norm_g, *, tile):
    tokens, d_model = x2.shape
    row = lambda i: (i, 0)
    const = lambda i: (0, 0)
    return pl.pallas_call(
        _outproj_kernel,
        grid=(tokens // tile,),
        in_specs=[pl.BlockSpec((tile, d_model), row),
                  pl.BlockSpec((tile, a_out.shape[-1]), row),
                  pl.BlockSpec((tile, b_out.shape[-1]), row),
                  pl.BlockSpec((tile, p2.shape[-1]), row),
                  pl.BlockSpec(w_out.shape, const),
                  pl.BlockSpec((1, d_model), const),
                  pl.BlockSpec(ple_gate_w.shape, const),
                  pl.BlockSpec(ple_proj_w.shape, const),
                  pl.BlockSpec((1, d_model), const)],
        out_specs=pl.BlockSpec((tile, d_model), row),
        out_shape=jax.ShapeDtypeStruct((tokens, d_model), F32),
        compiler_params=pltpu.CompilerParams(dimension_semantics=("arbitrary",),
                                             vmem_limit_bytes=VMEM_LIMIT),
        name="out_projection",
    )(x2, a_out, b_out, p2, w_out, ple_norm_g, ple_gate_w, ple_proj_w, final_norm_g)


def _lane_row(vec, offset):
    return jnp.zeros((1, LANES), F32).at[0, offset:offset + vec.shape[0]].set(vec.astype(F32))


def kernel(x, p, norm_g, w_in, sgu_ln_g, sgu_ln_b, sgu_w_s, sgu_b_s, dn_conv_w, dn_a_log, dn_dt_bias,
           dn_o_norm_g, w_out, ple_norm_g, ple_gate_w, ple_proj_w, final_norm_g):
    batch, seq, d_model = x.shape
    assert p.shape[0] == 1, "single-layer trunk: the last call also applies the final norm"
    tokens = batch * seq
    sgu_w = sgu_ln_g.shape[-1]
    dn_w = DN_HEADS * DN_HEAD_DIM
    main_cols = 3 * sgu_w + 4 * dn_w
    assert w_in.shape[-1] == main_cols + 2 * DN_HEADS
    proj_tile = min(PROJ_TILE, seq)
    dn_tile = min(DN_TILE, seq)

    x2 = x.reshape(tokens, d_model)
    w_main = w_in[0, :, :main_cols].astype(BF16)
    w_ba = jnp.zeros((d_model, LANES), BF16).at[:, :2 * DN_HEADS].set(w_in[0, :, main_cols:].astype(BF16))
    a_out, qkv, bz, ba = _in_projection(
        x2, norm_g[0][None, :], w_main, w_ba, sgu_ln_g[0][None, :], sgu_ln_b[0][None, :], sgu_w_s[0],
        sgu_b_s[0].T, dn_conv_w[0], tile=proj_tile, seq=seq)
    b_out = _deltanet(qkv.reshape(batch, seq, 3 * dn_w), bz.reshape(batch, seq, dn_w),
                      ba.reshape(batch, seq, LANES), _lane_row(dn_a_log[0], DN_HEADS),
                      _lane_row(dn_dt_bias[0], DN_HEADS), dn_o_norm_g[0][None, :], tile=dn_tile)
    out = _out_projection(x2, a_out, b_out.reshape(tokens, dn_w), p[0].reshape(tokens, -1),
                          w_out[0].astype(BF16), ple_norm_g[0][None, :], ple_gate_w[0].astype(BF16),
                          ple_proj_w[0].astype(BF16), final_norm_g[None, :], tile=min(OUT_TILE, tokens))
    return out.reshape(batch, seq, d_model)
```

```python
import functools

import jax
import jax.numpy as jnp
from jax import lax
from jax.experimental import pallas as pl
from jax.experimental.pallas import tpu as pltpu

F32 = jnp.float32
BF16 = jnp.bfloat16

EPS = 1e-6
LANES = 128
SGU_GROUPS = 4
SGU_CHUNK = 128
DN_HEADS = 4
DN_HEAD_DIM = 128
CONV_K = 4
DN_BLOCK = 128
INV_BASE = 8
TAIL_ROWS = 8
VMEM_LIMIT = 56 * 1024 * 1024
PROJ_TILE = 1024
OUT_TILE = 1024
DN_TILE = 1024

_SQRT_HALF = 0.7071067811865476


def _sigmoid(x):
    return 1.0 / (1.0 + jnp.exp(-x))


def _silu(x):
    return x * _sigmoid(x)


def _gelu(x):
    return 0.5 * x * (1.0 + lax.erf(x * _SQRT_HALF))


def _softplus(x):
    return jnp.maximum(x, 0.0) + jnp.log1p(jnp.exp(-jnp.abs(x)))


def _rms_norm(x, g):
    return x * lax.rsqrt(jnp.mean(x * x, axis=-1, keepdims=True) + EPS) * g


def _dot_exact_lhs(lhs_bf16, x):
    hi = x.astype(BF16)
    r1 = x - hi.astype(F32)
    mid = r1.astype(BF16)
    lo = (r1 - mid.astype(F32)).astype(BF16)
    d = lambda t: jnp.dot(lhs_bf16, t, preferred_element_type=F32)
    return d(hi) + d(mid) + d(lo)


def _bmm(a, b):
    return lax.dot_general(a.astype(BF16), b.astype(BF16), (((2,), (1,)), ((0,), (0,))),
                           preferred_element_type=F32)


def _bmm_nt(a, b):
    return lax.dot_general(a.astype(BF16), b.astype(BF16), (((2,), (2,)), ((0,), (0,))),
                           preferred_element_type=F32)


def _sgu_tile(u_all, v_all, z_all, lng_ref, lnb_ref, ws_ref, bst_ref, o_ref):
    rows = u_all.shape[0]
    r = lax.broadcasted_iota(jnp.int32, (SGU_CHUNK, SGU_CHUNK), 0)
    c = lax.broadcasted_iota(jnp.int32, (SGU_CHUNK, SGU_CHUNK), 1)
    causal = r >= c
    for g in range(SGU_GROUPS):
        lanes = slice(g * LANES, (g + 1) * LANES)
        w = jnp.where(causal, ws_ref[g], 0.0).astype(BF16)
        bias = bst_ref[:, g:g + 1]
        ln_g = lng_ref[:, lanes]
        ln_b = lnb_ref[:, lanes]
        for ci in range(rows // SGU_CHUNK):
            rs = slice(ci * SGU_CHUNK, (ci + 1) * SGU_CHUNK)
            v = _gelu(v_all[rs, lanes])
            mu = jnp.mean(v, axis=-1, keepdims=True)
            vc = v - mu
            y = vc * lax.rsqrt(jnp.mean(vc * vc, axis=-1, keepdims=True) + EPS) * ln_g + ln_b
            s = jnp.dot(w, y.astype(BF16), preferred_element_type=F32) + bias
            o_ref[rs, lanes] = (_gelu(u_all[rs, lanes]) * s * _silu(z_all[rs, lanes])).astype(o_ref.dtype)


def _conv_tile(qkv, convw_ref, xbuf_ref, ybuf_ref, o_ref):
    ts = qkv.shape[0]
    half = ts // 2
    for j in range(3 * DN_HEADS):
        lanes = slice(j * LANES, (j + 1) * LANES)
        x = qkv[:, lanes]
        xbuf_ref[j, TAIL_ROWS:TAIL_ROWS + ts, :] = x
        first = TAIL_ROWS - (CONV_K - 1)
        views = [xbuf_ref[j, pl.ds(first + s, half, stride=2), :] for s in range(CONV_K + 1)]
        for parity in range(2):
            y = None
            for k in range(CONV_K):
                term = convw_ref[k:k + 1, lanes] * views[parity + k]
                y = term if y is None else y + term
            y = _silu(y)
            if j < 2 * DN_HEADS:
                y = y * lax.rsqrt(jnp.sum(y * y, axis=-1, keepdims=True) + EPS)
            if j < DN_HEADS:
                y = y * (DN_HEAD_DIM ** -0.5)
            ybuf_ref[j, pl.ds(parity, half, stride=2), :] = y
        o_ref[:, lanes] = ybuf_ref[j].astype(o_ref.dtype)
        xbuf_ref[j, 0:TAIL_ROWS, :] = x[ts - TAIL_ROWS:ts, :]


def _inproj_kernel(tiles_per_seq, x_ref, g_ref, w_ref, wba_ref, lng_ref, lnb_ref, ws_ref, bst_ref, convw_ref,
                   a_ref, qkv_ref, bz_ref, ba_ref, xbuf_ref, ybuf_ref):
    sgu_w = a_ref.shape[-1]
    qkv_w = qkv_ref.shape[-1]

    @pl.when(pl.program_id(0) % tiles_per_seq == 0)
    def _():
        xbuf_ref[:, 0:TAIL_ROWS, :] = jnp.zeros((xbuf_ref.shape[0], TAIL_ROWS, LANES), F32)

    xn = _rms_norm(x_ref[...], g_ref[...]).astype(BF16)
    proj = lambda lo, hi: jnp.dot(xn, w_ref[:, lo:hi], preferred_element_type=F32)
    _conv_tile(proj(3 * sgu_w, 3 * sgu_w + qkv_w), convw_ref, xbuf_ref, ybuf_ref, qkv_ref)
    _sgu_tile(proj(0, sgu_w), proj(sgu_w, 2 * sgu_w), proj(2 * sgu_w, 3 * sgu_w),
              lng_ref, lnb_ref, ws_ref, bst_ref, a_ref)
    bz_ref[...] = proj(3 * sgu_w + qkv_w, 3 * sgu_w + qkv_w + bz_ref.shape[-1]).astype(bz_ref.dtype)
    ba_ref[...] = jnp.dot(xn, wba_ref[...], preferred_element_type=F32)


def _in_projection(x2, norm_g, w_main, w_ba, ln_g, ln_b, w_s, b_s_t, conv_w, *, tile, seq):
    tokens, d_model = x2.shape
    sgu_w = ln_g.shape[-1]
    qkv_w = conv_w.shape[-1]
    dn_w = qkv_w // 3
    assert seq % tile == 0 and tile % SGU_CHUNK == 0
    row = lambda i: (i, 0)
    const = lambda i: (0, 0)
    widths = (sgu_w, qkv_w, dn_w)
    out_shapes = [jax.ShapeDtypeStruct((tokens, w), BF16) for w in widths]
    out_shapes.append(jax.ShapeDtypeStruct((tokens, LANES), F32))
    out_specs = [pl.BlockSpec((tile, w), row) for w in widths] + [pl.BlockSpec((tile, LANES), row)]
    return pl.pallas_call(
        functools.partial(_inproj_kernel, seq // tile),
        grid=(tokens // tile,),
        in_specs=[pl.BlockSpec((tile, d_model), row),
                  pl.BlockSpec((1, d_model), const),
                  pl.BlockSpec(w_main.shape, const),
                  pl.BlockSpec(w_ba.shape, const),
                  pl.BlockSpec((1, sgu_w), const),
                  pl.BlockSpec((1, sgu_w), const),
                  pl.BlockSpec(w_s.shape, lambda i: (0, 0, 0)),
                  pl.BlockSpec(b_s_t.shape, const),
                  pl.BlockSpec(conv_w.shape, const)],
        out_specs=out_specs,
        out_shape=out_shapes,
        scratch_shapes=[pltpu.VMEM((qkv_w // LANES, TAIL_ROWS + tile, LANES), F32),
                        pltpu.VMEM((qkv_w // LANES, tile, LANES), F32)],
        compiler_params=pltpu.CompilerParams(dimension_semantics=("arbitrary",),
                                             vmem_limit_bytes=VMEM_LIMIT),
        name="in_projection",
    )(x2, norm_g, w_main, w_ba, ln_g, ln_b, w_s, b_s_t, conv_w)


def _unit_lower_inverse(strict_lower):
    n = strict_lower.shape[-1]
    r = lax.broadcasted_iota(jnp.int32, (n, n), 0)
    c = lax.broadcasted_iota(jnp.int32, (n, n), 1)
    same_block = lambda size: (r // size) == (c // size)
    m = -jnp.where(same_block(INV_BASE)[None], strict_lower, 0.0)
    inv = jnp.where(r == c, 1.0, 0.0)[None] + m
    power = m
    span = 2
    while span < INV_BASE:
        power = _bmm(power, power)
        inv = inv + _bmm(inv, power)
        span *= 2
    size = INV_BASE
    while size < n:
        coupling = jnp.where((same_block(2 * size) & ~same_block(size))[None], strict_lower, 0.0)
        inv = inv - _bmm(_bmm(inv, coupling), inv)
        size *= 2
    return inv


def _deltanet_kernel(qkv_ref, z_ref, ba_ref, alog_ref, dtb_ref, og_ref, o_ref, state_ref):
    ts = qkv_ref.shape[1]
    n = DN_BLOCK

    @pl.when(pl.program_id(1) == 0)
    def _():
        state_ref[...] = jnp.zeros(state_ref.shape, F32)

    ba = ba_ref[0]
    beta = _sigmoid(ba)
    g = -jnp.exp(alog_ref[...]) * _softplus(ba + dtb_ref[...])

    r = lax.broadcasted_iota(jnp.int32, (n, n), 0)
    c = lax.broadcasted_iota(jnp.int32, (n, n), 1)
    lower = r >= c
    strict = r > c
    tri_ones = jnp.where(lower, 1.0, 0.0).astype(BF16)

    nblk = ts // n
    gcs = [_dot_exact_lhs(tri_ones, g[ci * n:(ci + 1) * n, :]) for ci in range(nblk)]
    gcs_t = [gc.T for gc in gcs]
    units = [(ci, h) for ci in range(nblk) for h in range(DN_HEADS)]
    stack = lambda f: jnp.stack([f(ci, h) for ci, h in units])
    act = lambda ci, col: qkv_ref[0, ci * n:(ci + 1) * n, col * LANES:(col + 1) * LANES]
    q_bf = stack(lambda ci, h: act(ci, h))
    k_bf = stack(lambda ci, h: act(ci, DN_HEADS + h))
    q = q_bf.astype(F32)
    k = k_bf.astype(F32)
    v = stack(lambda ci, h: act(ci, 2 * DN_HEADS + h)).astype(F32)
    full = (len(units), n, LANES)
    b_full = jnp.broadcast_to(stack(lambda ci, h: beta[ci * n:(ci + 1) * n, h:h + 1]), full)
    g_full = jnp.broadcast_to(stack(lambda ci, h: gcs[ci][:, DN_HEADS + h:DN_HEADS + h + 1]), full)
    g_row = stack(lambda ci, h: gcs_t[ci][DN_HEADS + h:DN_HEADS + h + 1, :])
    g_last = g_full[:, n - 1:n, :]
    decay = jnp.exp(jnp.where(lower[None], g_full - g_row, -jnp.inf))
    eg = jnp.exp(g_full)

    k_beta = k * b_full
    kk = _bmm_nt(k_beta, k_bf) * decay
    inv = _unit_lower_inverse(jnp.where(strict[None], kk, 0.0))
    wu = _bmm(inv, jnp.concatenate([k_beta * eg, v * b_full], axis=2))
    qk = _bmm_nt(q_bf, k_bf) * decay
    k_dec = k * jnp.exp(g_last - g_full)
    k_dec_t = jnp.stack([k_dec[u].T for u in range(len(units))])
    state_decay = jnp.exp(g_last)

    kd_wu = _bmm(k_dec_t, wu)
    qk_wu = _bmm(qk, wu)
    state_in = jnp.concatenate([-kd_wu[:, :, 0:LANES], q * eg - qk_wu[:, :, 0:LANES]], axis=1)
    state_add = kd_wu[:, :, LANES:2 * LANES]
    out_add = qk_wu[:, :, LANES:2 * LANES]

    state = state_ref[...]
    for ci in range(nblk):
        us = slice(ci * DN_HEADS, (ci + 1) * DN_HEADS)
        prod = _bmm(state_in[us], state)
        out = prod[:, n:2 * n, :] + out_add[us]
        state = state * state_decay[us] + prod[:, 0:n, :] + state_add[us]
        for h in range(DN_HEADS):
            rs = slice(ci * n, (ci + 1) * n)
            o = _rms_norm(out[h], og_ref[...])
            z = z_ref[0, rs, h * LANES:(h + 1) * LANES].astype(F32)
            o_ref[0, rs, h * LANES:(h + 1) * LANES] = (o * _silu(z)).astype(o_ref.dtype)
    state_ref[...] = state


def _deltanet(qkv, bz, ba, a_log_row, dt_bias_row, o_norm_g, *, tile):
    batch, seq, qkv_w = qkv.shape
    width = bz.shape[-1]
    assert seq % tile == 0 and tile % DN_BLOCK == 0
    assert DN_BLOCK == LANES == DN_HEAD_DIM
    blk = lambda b, t: (b, t, 0)
    const2 = lambda b, t: (0, 0)
    return pl.pallas_call(
        _deltanet_kernel,
        grid=(batch, seq // tile),
        in_specs=[pl.BlockSpec((1, tile, qkv_w), blk),
                  pl.BlockSpec((1, tile, width), blk),
                  pl.BlockSpec((1, tile, LANES), blk),
                  pl.BlockSpec((1, LANES), const2),
                  pl.BlockSpec((1, LANES), const2),
                  pl.BlockSpec((1, DN_HEAD_DIM), const2)],
        out_specs=pl.BlockSpec((1, tile, width), blk),
        out_shape=jax.ShapeDtypeStruct((batch, seq, width), BF16),
        scratch_shapes=[pltpu.VMEM((DN_HEADS, DN_HEAD_DIM, DN_HEAD_DIM), F32)],
        compiler_params=pltpu.CompilerParams(dimension_semantics=("arbitrary", "arbitrary"),
                                             vmem_limit_bytes=VMEM_LIMIT),
        name="deltanet",
    )(qkv, bz, ba, a_log_row, dt_bias_row, o_norm_g)


def _outproj_kernel(x_ref, a_ref, b_ref, p_ref, wo_ref, pg_ref, wg_ref, wp_ref, fg_ref, o_ref):
    half = a_ref.shape[-1]
    h = x_ref[...]
    h = h + jnp.dot(a_ref[...], wo_ref[0:half, :], preferred_element_type=F32)
    h = h + jnp.dot(b_ref[...], wo_ref[half:2 * half, :], preferred_element_type=F32)
    hn = _rms_norm(h, pg_ref[...]).astype(BF16)
    gate = _sigmoid(jnp.dot(hn, wg_ref[...], preferred_element_type=F32))
    pe = jnp.dot(p_ref[...].astype(BF16), wp_ref[...], preferred_element_type=F32)
    h = h + gate * pe
    o_ref[...] = _rms_norm(h, fg_ref[...])


def _out_projection(x2, a_out, b_out, p2, w_out, ple_norm_g, ple_gate_w, ple_proj_w, final_norm_g, *, tile):
    tokens, d_model = x2.shape
    row = lambda i: (i, 0)
    const = lambda i: (0, 0)
    return pl.pallas_call(
        _outproj_kernel,
        grid=(tokens // tile,),
        in_specs=[pl.BlockSpec((tile, d_model), row),
                  pl.BlockSpec((tile, a_out.shape[-1]), row),
                  pl.BlockSpec((tile, b_out.shape[-1]), row),
                  pl.BlockSpec((tile, p2.shape[-1]), row),
                  pl.BlockSpec(w_out.shape, const),
                  pl.BlockSpec((1, d_model), const),
                  pl.BlockSpec(ple_gate_w.shape, const),
                  pl.BlockSpec(ple_proj_w.shape, const),
                  pl.BlockSpec((1, d_model), const)],
        out_specs=pl.BlockSpec((tile, d_model), row),
        out_shape=jax.ShapeDtypeStruct((tokens, d_model), F32),
        compiler_params=pltpu.CompilerParams(dimension_semantics=("arbitrary",),
                                             vmem_limit_bytes=VMEM_LIMIT),
        name="out_projection",
    )(x2, a_out, b_out, p2, w_out, ple_norm_g, ple_gate_w, ple_proj_w, final_norm_g)


def _lane_row(vec, offset):
    return jnp.zeros((1, LANES), F32).at[0, offset:offset + vec.shape[0]].set(vec.astype(F32))


def kernel(x, p, norm_g, w_in, sgu_ln_g, sgu_ln_b, sgu_w_s, sgu_b_s, dn_conv_w, dn_a_log, dn_dt_bias,
           dn_o_norm_g, w_out, ple_norm_g, ple_gate_w, ple_proj_w, final_norm_g):
    batch, seq, d_model = x.shape
    assert p.shape[0] == 1, "single-layer trunk: the last call also applies the final norm"
    tokens = batch * seq
    sgu_w = sgu_ln_g.shape[-1]
    dn_w = DN_HEADS * DN_HEAD_DIM
    main_cols = 3 * sgu_w + 4 * dn_w
    assert w_in.shape[-1] == main_cols + 2 * DN_HEADS
    proj_tile = min(PROJ_TILE, seq)
    dn_tile = min(DN_TILE, seq)

    x2 = x.reshape(tokens, d_model)
    w_main = w_in[0, :, :main_cols].astype(BF16)
    w_ba = jnp.zeros((d_model, LANES), BF16).at[:, :2 * DN_HEADS].set(w_in[0, :, main_cols:].astype(BF16))
    a_out, qkv, bz, ba = _in_projection(
        x2, norm_g[0][None, :], w_main, w_ba, sgu_ln_g[0][None, :], sgu_ln_b[0][None, :], sgu_w_s[0],
        sgu_b_s[0].T, dn_conv_w[0], tile=proj_tile, seq=seq)
    b_out = _deltanet(qkv.reshape(batch, seq, 3 * dn_w), bz.reshape(batch, seq, dn_w),
                      ba.reshape(batch, seq, LANES), _lane_row(dn_a_log[0], DN_HEADS),
                      _lane_row(dn_dt_bias[0], DN_HEADS), dn_o_norm_g[0][None, :], tile=dn_tile)
    out = _out_projection(x2, a_out, b_out.reshape(tokens, dn_w), p[0].reshape(tokens, -1),
                          w_out[0].astype(BF16), ple_norm_g[0][None, :], ple_gate_w[0].astype(BF16),
                          ple_proj_w[0].astype(BF16), final_norm_g[None, :], tile=min(OUT_TILE, tokens))
    return out.reshape(batch, seq, d_model)
```
